```python
import math
import jax, jax.numpy as jnp
from jax import lax
import numpy as np

D_MODEL = 1024
BATCH = 32
SEQ = 2048
DEPTH = 1
DEC_BATCH = 16
DEC_SEQ = 16
PAST_LEN = 1024

CHUNK = 64
N_META = 16
Q_BLOCK = 128
EPS = 1e-6
NEG = -1e30
FOX_HEADS = 8
FOX_HEAD_DIM = 64
FOX_W = FOX_HEADS * FOX_HEAD_DIM
FOX_SCALE = 1.0 / math.sqrt(FOX_HEAD_DIM)
MLA_HEADS = 8
MLA_NOPE = 64
MLA_ROPE = 32
MLA_V = 64
MLA_QK = MLA_NOPE + MLA_ROPE
MLA_Q_LORA = 384
MLA_KV_LORA = 256
MLA_SCALE = 1.0 / math.sqrt(MLA_QK)
ROPE_BASE = 10000.0
D_FF = 2816
CONV_W = 3
OFF_FK = FOX_W
OFF_FV = 2 * FOX_W
OFF_FF = 3 * FOX_W
OFF_CQ = OFF_FF + FOX_HEADS
OFF_CKV = OFF_CQ + MLA_Q_LORA
OFF_KR = OFF_CKV + MLA_KV_LORA
OFF_GATE = OFF_KR + MLA_ROPE
IN_WIDTH = OFF_GATE + 2 * D_MODEL

kernel_name = 'fox_mla_gated_hybrid_stream_step'


def rmsnorm(x, g):
    xf = x.astype(jnp.float32)
    y = xf * lax.rsqrt(jnp.mean(xf * xf, axis=-1, keepdims=True) + EPS)
    return (y * g.astype(jnp.float32)).astype(x.dtype)


def rope_tables(pos, dtype):
    inv = ROPE_BASE ** (-jnp.arange(0, MLA_ROPE, 2, dtype=jnp.float32) / MLA_ROPE)
    ang = pos.astype(jnp.float32)[:, None] * inv[None, :]
    return jnp.cos(ang).astype(dtype), jnp.sin(ang).astype(dtype)


def apply_rope(x, cos, sin):
    x1, x2 = jnp.split(x, 2, axis=-1)
    return jnp.concatenate([x1 * cos - x2 * sin, x1 * sin + x2 * cos], axis=-1)


def mixer_inputs(xn, pos, lw):
    B, S, _ = xn.shape
    proj = xn @ lw['w_in']
    fq, fk, fv, ff, cq, ckv, kr, gl = jnp.split(
        proj, [OFF_FK, OFF_FV, OFF_FF, OFF_CQ, OFF_CKV, OFF_KR, OFF_GATE], axis=-1)
    fox_q = fq.reshape(B, S, FOX_HEADS, FOX_HEAD_DIM)
    fox_k = fk.reshape(B, S, FOX_HEADS, FOX_HEAD_DIM)
    fox_v = fv.reshape(B, S, FOX_HEADS, FOX_HEAD_DIM)
    fox_logf = jax.nn.log_sigmoid(ff.astype(jnp.float32) + lw['b_forget'].astype(jnp.float32))
    cos, sin = rope_tables(pos, xn.dtype)
    q = (rmsnorm(cq, lw['mla_q_norm_g']) @ lw['w_uq']).reshape(B, S, MLA_HEADS, MLA_QK)
    q_nope, q_rope = jnp.split(q, [MLA_NOPE], axis=-1)
    mla_q = jnp.concatenate([q_nope, apply_rope(q_rope, cos[:, None, :], sin[:, None, :])], axis=-1)
    ckv_n = rmsnorm(ckv, lw['mla_kv_norm_g'])
    k_rope = apply_rope(kr, cos, sin)
    gate_a, gate_b = jnp.split(jax.nn.sigmoid(gl), 2, axis=-1)
    return fox_q, fox_k, fox_v, fox_logf, mla_q, ckv_n, k_rope, gate_a, gate_b


def mla_keys_values(ckv_n, k_rope, w_ukv):
    B, S, _ = ckv_n.shape
    kv = (ckv_n @ w_ukv).reshape(B, S, MLA_HEADS, MLA_NOPE + MLA_V)
    k_nope, v = jnp.split(kv, [MLA_NOPE], axis=-1)
    k_pe = jnp.broadcast_to(k_rope[:, :, None, :], (B, S, MLA_HEADS, MLA_ROPE))
    return jnp.concatenate([k_nope, k_pe], axis=-1), v


def attend(q, k, v, bias, scale):
    s = jnp.einsum('bqhd,bkhd->bhqk', q, k).astype(jnp.float32) * scale + bias
    p = jax.nn.softmax(s, axis=-1).astype(v.dtype)
    return jnp.einsum('bhqk,bkhd->bqhd', p, v)


def sweep_query_blocks(q, k, v, bias_for_block, scale):
    B, Lp, H, _ = q.shape
    starts = jnp.arange(Lp // Q_BLOCK) * Q_BLOCK

    def one(start):
        qb = lax.dynamic_slice_in_dim(q, start, Q_BLOCK, axis=1)
        return attend(qb, k, v, bias_for_block(start), scale)

    out = lax.map(one, starts)
    return jnp.moveaxis(out, 0, 1).reshape(B, Lp, H, v.shape[-1])


def merge_branches(o_fox, o_mla, gate_a, gate_b, lw):
    B, S = o_fox.shape[:2]
    ya = o_fox.reshape(B, S, FOX_W) @ lw['w_o_fox']
    yb = o_mla.reshape(B, S, MLA_HEADS * MLA_V) @ lw['w_o_mla']
    return (gate_a * ya + gate_b * yb) @ lw['w_out']


def conv_ffn(xn, left, lw):
    u = xn @ lw['w_up']
    S = u.shape[1]
    up = jnp.concatenate([left.astype(u.dtype), u], axis=1)
    c = lw['conv_b']
    for j in range(CONV_W):
        c = c + up[:, j:j + S] * lw['conv_w'][j]
    gate, val = jnp.split(c, 2, axis=-1)
    return (jax.nn.silu(gate) * val) @ lw['w_down'], up[:, up.shape[1] - (CONV_W - 1):]


def prompt_layer(h, lw):
    B, L, _ = h.shape
    Lp = -(-L // Q_BLOCK) * Q_BLOCK
    pad = Lp - L
    xn = rmsnorm(h, lw['norm_mix_g'])
    fq, fk, fv, flogf, mq, ckv_n, k_rope, ga, gb = mixer_inputs(xn, jnp.arange(L), lw)
    padseq = lambda a: jnp.pad(a, [(0, 0), (0, pad)] + [(0, 0)] * (a.ndim - 2))
    idx = jnp.arange(Lp)
    F = jnp.swapaxes(jnp.cumsum(padseq(flogf), axis=1), 1, 2)

    def fox_bias(start):
        tq = start + jnp.arange(Q_BLOCK)
        Fq = lax.dynamic_slice_in_dim(F, start, Q_BLOCK, axis=2)
        causal = (idx[None, :] <= tq[:, None])[None, None]
        return jnp.where(causal, Fq[..., None] - F[:, :, None, :], NEG)

    o_fox = sweep_query_blocks(padseq(fq), padseq(fk), padseq(fv), fox_bias, FOX_SCALE)[:, :L]
    k_mla, v_mla = mla_keys_values(ckv_n, k_rope, lw['w_ukv'])
    cid = jnp.where(idx < N_META, 0, 1 + (idx - N_META) // CHUNK)
    key_ok = idx < L

    def mla_bias(start):
        cq = lax.dynamic_slice_in_dim(cid, start, Q_BLOCK)
        ok = (cid[None, :] <= cq[:, None]) & key_ok[None, :]
        return jnp.where(ok, 0.0, NEG)[None, None]

    o_mla = sweep_query_blocks(padseq(mq), padseq(k_mla), padseq(v_mla), mla_bias, MLA_SCALE)[:, :L]
    h = h + merge_branches(o_fox, o_mla, ga, gb, lw)
    left = jnp.zeros((B, CONV_W - 1, 2 * D_FF), h.dtype)
    f, conv_state = conv_ffn(rmsnorm(h, lw['norm_ffn_g']), left, lw)
    h = h + f
    return h, (fk, fv, flogf, ckv_n, k_rope, conv_state)


def sample_layer(h, ck, cv, clogf, cckv, ckr, cconv, lw):
    B, n, _ = h.shape
    P = ck.shape[1]
    xn = rmsnorm(h, lw['norm_mix_g'])
    fq, fk, fv, flogf, mq, ckv_n, k_rope, ga, gb = mixer_inputs(xn, P + jnp.arange(n), lw)
    k_all = jnp.concatenate([ck.astype(fk.dtype), fk], axis=1)
    v_all = jnp.concatenate([cv.astype(fv.dtype), fv], axis=1)
    F = jnp.swapaxes(jnp.cumsum(jnp.concatenate([clogf.astype(jnp.float32), flogf], axis=1), axis=1), 1, 2)
    j = jnp.arange(P + n)
    t = P + jnp.arange(n)
    causal = (j[None, :] <= t[:, None])[None, None]
    fox_bias = jnp.where(causal, F[:, :, P:, None] - F[:, :, None, :], NEG)
    o_fox = attend(fq, k_all, v_all, fox_bias, FOX_SCALE)
    ckv_all = jnp.concatenate([cckv.astype(ckv_n.dtype), ckv_n], axis=1)
    kr_all = jnp.concatenate([ckr.astype(k_rope.dtype), k_rope], axis=1)
    k_mla, v_mla = mla_keys_values(ckv_all, kr_all, lw['w_ukv'])
    o_mla = attend(mq, k_mla, v_mla, 0.0, MLA_SCALE)
    h = h + merge_branches(o_fox, o_mla, ga, gb, lw)
    f, conv_state = conv_ffn(rmsnorm(h, lw['norm_ffn_g']), cconv, lw)
    h = h + f
    return h, (fk, fv, flogf, ckv_n, k_rope, conv_state)


def setup_inputs(seed: int = 0) -> dict:
    key = jax.random.key(seed)
    ks = jax.random.split(key, 32)
    nrm = lambda k, shape, s=1.0: jax.random.normal(k, shape, jnp.float32) * s
    gain = lambda k, shape: 1.0 + 0.1 * jax.random.normal(k, shape, jnp.float32)
    return {
        'x_prompt': nrm(ks[0], (BATCH, SEQ, D_MODEL)),
        'x_sample': nrm(ks[1], (DEC_BATCH, DEC_SEQ, D_MODEL)),
        'cache_fox_k': nrm(ks[2], (DEPTH, DEC_BATCH, PAST_LEN, FOX_HEADS, FOX_HEAD_DIM)),
        'cache_fox_v': nrm(ks[3], (DEPTH, DEC_BATCH, PAST_LEN, FOX_HEADS, FOX_HEAD_DIM)),
        'cache_fox_logf': jax.nn.log_sigmoid(4.0 + nrm(ks[4], (DEPTH, DEC_BATCH, PAST_LEN, FOX_HEADS))),
        'cache_mla_ckv': nrm(ks[5], (DEPTH, DEC_BATCH, PAST_LEN, MLA_KV_LORA)),
        'cache_mla_krope': nrm(ks[6], (DEPTH, DEC_BATCH, PAST_LEN, MLA_ROPE)),
        'state_ffn_conv': nrm(ks[7], (DEPTH, DEC_BATCH, CONV_W - 1, 2 * D_FF)),
        'meta_tokens': nrm(ks[8], (N_META, D_MODEL)),
        'norm_mix_g': gain(ks[9], (DEPTH, D_MODEL)),
        'w_in': nrm(ks[10], (DEPTH, D_MODEL, IN_WIDTH), D_MODEL ** -0.5),
        'b_forget': 4.0 + nrm(ks[11], (DEPTH, FOX_HEADS)),
        'mla_q_norm_g': gain(ks[12], (DEPTH, MLA_Q_LORA)),
        'w_uq': nrm(ks[13], (DEPTH, MLA_Q_LORA, MLA_HEADS * MLA_QK), MLA_Q_LORA ** -0.5),
        'mla_kv_norm_g': gain(ks[14], (DEPTH, MLA_KV_LORA)),
        'w_ukv': nrm(ks[15], (DEPTH, MLA_KV_LORA, MLA_HEADS * (MLA_NOPE + MLA_V)), MLA_KV_LORA ** -0.5),
        'w_o_fox': nrm(ks[16], (DEPTH, FOX_W, D_MODEL), FOX_W ** -0.5),
        'w_o_mla': nrm(ks[17], (DEPTH, MLA_HEADS * MLA_V, D_MODEL), (MLA_HEADS * MLA_V) ** -0.5),
        'w_out': nrm(ks[18], (DEPTH, D_MODEL, D_MODEL), D_MODEL ** -0.5),
        'norm_ffn_g': gain(ks[19], (DEPTH, D_MODEL)),
        'w_up': nrm(ks[20], (DEPTH, D_MODEL, 2 * D_FF), D_MODEL ** -0.5),
        'conv_w': nrm(ks[21], (DEPTH, CONV_W, 2 * D_FF), CONV_W ** -0.5),
        'conv_b': nrm(ks[22], (DEPTH, 2 * D_FF), 0.02),
        'w_down': nrm(ks[23], (DEPTH, D_FF, D_MODEL), D_FF ** -0.5),
        'norm_final_g': gain(ks[24], (D_MODEL,)),
    }


def reference(x_prompt, x_sample, cache_fox_k, cache_fox_v, cache_fox_logf, cache_mla_ckv, cache_mla_krope,
              state_ffn_conv, meta_tokens, norm_mix_g, w_in, b_forget, mla_q_norm_g, w_uq, mla_kv_norm_g, w_ukv,
              w_o_fox, w_o_mla, w_out, norm_ffn_g, w_up, conv_w, conv_b, w_down, norm_final_g):
    B = x_prompt.shape[0]
    meta = jnp.broadcast_to(meta_tokens[None].astype(x_prompt.dtype), (B, N_META, D_MODEL))
    h_p = jnp.concatenate([meta, x_prompt], axis=1)
    h_s = x_sample
    new_p = []
    new_s = []
    for layer in range(DEPTH):
        lw = {
            'norm_mix_g': norm_mix_g[layer], 'w_in': w_in[layer], 'b_forget': b_forget[layer],
            'mla_q_norm_g': mla_q_norm_g[layer], 'w_uq': w_uq[layer], 'mla_kv_norm_g': mla_kv_norm_g[layer],
            'w_ukv': w_ukv[layer], 'w_o_fox': w_o_fox[layer], 'w_o_mla': w_o_mla[layer], 'w_out': w_out[layer],
            'norm_ffn_g': norm_ffn_g[layer], 'w_up': w_up[layer], 'conv_w': conv_w[layer],
            'conv_b': conv_b[layer], 'w_down': w_down[layer],
        }
        h_p, st_p = prompt_layer(h_p, lw)
        h_s, st_s = sample_layer(h_s, cache_fox_k[layer], cache_fox_v[layer], cache_fox_logf[layer],
                                 cache_mla_ckv[layer], cache_mla_krope[layer], state_ffn_conv[layer], lw)
        new_p.append(st_p)
        new_s.append(st_s)
    stk = lambda states, i: jnp.stack([s[i] for s in states], axis=0)
    y_prompt = rmsnorm(h_p, norm_final_g)[:, N_META:]
    y_sample = rmsnorm(h_s, norm_final_g)
    return (y_prompt, y_sample,
            stk(new_p, 0), stk(new_p, 1), stk(new_p, 2), stk(new_p, 3), stk(new_p, 4), stk(new_p, 5),
            stk(new_s, 0), stk(new_s, 1), stk(new_s, 2), stk(new_s, 3), stk(new_s, 4), stk(new_s, 5))
```

```python
import functools
import math

import jax
import jax.numpy as jnp
import numpy as np
from jax import lax
from jax.experimental import pallas as pl
from jax.experimental.pallas import tpu as pltpu

F32 = jnp.float32
BF16 = jnp.bfloat16

N_META = 16
CHUNK = 64
EPS = 1e-6
NEG = -1e30
FOX_HEADS = 8
FOX_HEAD_DIM = 64
MLA_HEADS = 8
MLA_NOPE = 64
MLA_ROPE = 32
MLA_V = 64
MLA_QK = MLA_NOPE + MLA_ROPE
ROPE_BASE = 10000.0
CONV_W = 3

FOX_W = FOX_HEADS * FOX_HEAD_DIM
FOX_SCALE = 1.0 / math.sqrt(FOX_HEAD_DIM)
MLA_SCALE = 1.0 / math.sqrt(MLA_QK)

LANES = 128
MXU_DIM = 256
VMEM_BYTES_V7X = 64 * 1024 * 1024
F_PARTS = 3
HEADS_PER_PAIR = LANES // FOX_HEAD_DIM
N_PAIRS = FOX_HEADS // HEADS_PER_PAIR
PAIR_W = 2 * LANES

C_FQ = 0
C_FK = C_FQ + FOX_W
C_FV = C_FK + FOX_W
C_FF = C_FV + FOX_W
C_CQ = C_FF + LANES


def _rms(x, g):
    return x * lax.rsqrt(jnp.mean(x * x, axis=-1, keepdims=True) + EPS) * g


def _split3(x):
    hi = x.astype(BF16)
    r = x - hi.astype(F32)
    mid = r.astype(BF16)
    lo = (r - mid.astype(F32)).astype(BF16)
    return hi, mid, lo


def _cumsum_rows(lf, tri, carry, cs):
    hi, mid, lo = _split3(lf)
    pieces = []
    for c in range(lf.shape[0] // cs):
        sl = slice(c * cs, (c + 1) * cs)
        fc = (jnp.dot(tri, hi[sl], preferred_element_type=F32)
              + jnp.dot(tri, mid[sl], preferred_element_type=F32)
              + jnp.dot(tri, lo[sl], preferred_element_type=F32)) + carry
        carry = fc[cs - 1:cs, :]
        pieces.append(fc)
    f = pieces[0] if len(pieces) == 1 else jnp.concatenate(pieces, axis=0)
    return f, carry


def _forget_columns(f):
    hi, mid, lo = _split3(f)
    lane = lax.broadcasted_iota(jnp.int32, f.shape, 1)
    zero = jnp.zeros_like(hi)
    return jnp.where(lane < FOX_HEADS, hi,
                     jnp.where(lane < 2 * FOX_HEADS, mid,
                               jnp.where(lane < F_PARTS * FOX_HEADS, lo, zero)))


def _store_pairs(ref, main, extra):
    for j in range(N_PAIRS):
        ref[0, :, PAIR_W * j:PAIR_W * j + LANES] = main[:, LANES * j:LANES * (j + 1)]
        ref[0, :, PAIR_W * j + LANES:PAIR_W * (j + 1)] = extra


def _rope(x, ct, sa, sb):
    return x * ct + pltpu.roll(x, LANES - MLA_ROPE // 2, 1) * sa + pltpu.roll(x, MLA_ROPE // 2, 1) * sb


def _proj_kernel(x_ref, ct_ref, sa_ref, sb_ref, finit_ref, gmix_ref, w_ref, bexp_ref, gq_ref, wuq_ref,
                 gkv_ref, wukv_ref, tri_ref,
                 fk_ref, fv_ref, logf_ref, ckv_ref, kr_ref, qf_ref, kfa_ref, vf_ref, qn_ref, qr_ref,
                 kc_ref, vm_ref, gate_ref, ftot_ref, carry_ref, *, cs, q_lora, kv_lora, d_model):
    t = pl.program_id(1)

    @pl.when(t == 0)
    def _():
        carry_ref[...] = finit_ref[0]

    c_ckv = C_CQ + q_lora
    c_kr = c_ckv + kv_lora
    c_gate = c_kr + LANES
    xb = _rms(x_ref[0], gmix_ref[...]).astype(BF16)

    def proj(a, b):
        return jnp.dot(xb, w_ref[:, a:b], preferred_element_type=F32)

    qf_ref[0] = proj(C_FQ, C_FK).astype(BF16)
    fk = proj(C_FK, C_FV)
    fk_ref[0] = fk
    fv = proj(C_FV, C_FF)
    fv_ref[0] = fv
    vf_ref[0] = fv.astype(BF16)
    lf = jax.nn.log_sigmoid(proj(C_FF, C_CQ) + bexp_ref[...])
    logf_ref[0] = lf[:, :FOX_HEADS]
    f, carry = _cumsum_rows(lf, tri_ref[...], carry_ref[0:1, :], cs)
    carry_ref[0:1, :] = carry
    ftot_ref[0] = jnp.broadcast_to(carry, ftot_ref.shape[1:])
    _store_pairs(kfa_ref, fk.astype(BF16), _forget_columns(f))

    ct, sa, sb = ct_ref[...], sa_ref[...], sb_ref[...]
    cqn = _rms(proj(C_CQ, c_ckv), gq_ref[...]).astype(BF16)
    q = jnp.dot(cqn, wuq_ref[...], preferred_element_type=F32) * MLA_SCALE
    nope_w = MLA_HEADS * MLA_NOPE
    qn_ref[0] = q[:, :nope_w].astype(BF16)
    for c in range(MLA_HEADS * MLA_ROPE // LANES):
        qr_ref[0, :, LANES * c:LANES * (c + 1)] = _rope(
            q[:, nope_w + LANES * c:nope_w + LANES * (c + 1)], ct, sa, sb).astype(BF16)
    ckvn = _rms(proj(c_ckv, c_kr), gkv_ref[...])
    ckv_ref[0] = ckvn
    kv = jnp.dot(ckvn.astype(BF16), wukv_ref[...], preferred_element_type=F32)
    vm_ref[0] = kv[:, nope_w:].astype(BF16)
    kr4 = _rope(proj(c_kr, c_gate), ct, sa, sb)
    kr_ref[0] = kr4[:, :MLA_ROPE]
    _store_pairs(kc_ref, kv[:, :nope_w].astype(BF16), kr4.astype(BF16))

    gate_ref[0] = jax.nn.sigmoid(proj(c_gate, c_gate + 2 * d_model))


def _const_spec(shape):
    return pl.BlockSpec(shape, lambda *_: (0,) * len(shape), pipeline_mode=pl.Buffered(1))


def _vmem_limit(block_bytes, const_bytes, temp_bytes):
    need = 2 * block_bytes + const_bytes + temp_bytes
    assert need < VMEM_BYTES_V7X, need
    return int(need)


def _project(x, tables, finit, wts, tri, *, ts, cs):
    B, n, d = x.shape
    assert n % ts == 0 and ts % cs == 0
    nt = n // ts
    q_lora = wts['g_q'].shape[1]
    kv_lora = wts['g_kv'].shape[1]
    ct, sa, sb = tables
    row = lambda w: pl.BlockSpec((1, ts, w), lambda b, t: (b, t, 0))
    tab = pl.BlockSpec((ts, LANES), lambda b, t: (t, 0))
    in_specs = [row(d), tab, tab, tab,
                pl.BlockSpec((1, 8, LANES), lambda b, t: (0, 0, 0)),
                _const_spec(wts['g_mix'].shape), _const_spec(wts['w_a'].shape), _const_spec(wts['b_exp'].shape),
                _const_spec(wts['g_q'].shape), _const_spec(wts['w_uq'].shape), _const_spec(wts['g_kv'].shape),
                _const_spec(wts['w_ukv'].shape), _const_spec(tri.shape)]
    widths = dict(fk=(FOX_W, F32), fv=(FOX_W, F32), logf=(FOX_HEADS, F32), ckv=(kv_lora, F32), kr=(MLA_ROPE, F32),
                  qf=(FOX_W, BF16), kfa=(N_PAIRS * PAIR_W, BF16), vf=(FOX_W, BF16),
                  qn=(MLA_HEADS * MLA_NOPE, BF16), qr=(MLA_HEADS * MLA_ROPE, BF16),
                  kc=(N_PAIRS * PAIR_W, BF16), vm=(MLA_HEADS * MLA_V, BF16), gate=(2 * d, F32))
    out_shape = [jax.ShapeDtypeStruct((B, n, w), dt) for w, dt in widths.values()]
    out_specs = [row(w) for w, _ in widths.values()]
    out_shape.append(jax.ShapeDtypeStruct((B, 8, LANES), F32))
    out_specs.append(pl.BlockSpec((1, 8, LANES), lambda b, t: (b, 0, 0)))
    block_bytes = ts * (d * 4 + 3 * LANES * 4 + sum(w * jnp.dtype(dt).itemsize for w, dt in widths.values()))
    const_bytes = sum(int(np.prod(wts[k].shape)) * wts[k].dtype.itemsize for k in ('w_a', 'w_uq', 'w_ukv')) + tri.size * 2
    temp_bytes = ts * (2 * d + 2 * d) * 4 * 2
    outs = pl.pallas_call(
        functools.partial(_proj_kernel, cs=cs, q_lora=q_lora, kv_lora=kv_lora, d_model=d),
        grid=(B, nt), in_specs=in_specs, out_specs=out_specs, out_shape=out_shape,
        scratch_shapes=[pltpu.VMEM((8, LANES), F32)],
        compiler_params=pltpu.CompilerParams(
            dimension_semantics=("arbitrary", "arbitrary"),
            vmem_limit_bytes=_vmem_limit(block_bytes, const_bytes, temp_bytes)),
        name="proj",
    )(x, ct, sa, sb, finit, wts['g_mix'], wts['w_a'], wts['b_exp'], wts['g_q'], wts['w_uq'], wts['g_kv'],
      wts['w_ukv'], tri)
    res = dict(zip(widths.keys(), outs[:-1]))
    res['ftot'] = outs[-1]
    return res


def _prefix_kernel(ck_ref, cv_ref, lf_ref, ckv_ref, kr4_ref, wukv_ref, tri_ref,
                   kfa_ref, vf_ref, kc_ref, vm_ref, *, cs):
    f, total = _cumsum_rows(lf_ref[0], tri_ref[...], jnp.zeros((1, LANES), F32), cs)
    _store_pairs(kfa_ref, ck_ref[0].astype(BF16), _forget_columns(f - total))
    vf_ref[0] = cv_ref[0].astype(BF16)
    kv = jnp.dot(ckv_ref[0].astype(BF16), wukv_ref[...], preferred_element_type=F32)
    nope_w = MLA_HEADS * MLA_NOPE
    vm_ref[0] = kv[:, nope_w:].astype(BF16)
    _store_pairs(kc_ref, kv[:, :nope_w].astype(BF16), kr4_ref[0].astype(BF16))


def _prefix_operands(ck, cv, lf_exp, cckv, kr4, w_ukv, tri, *, cs):
    B, P, _ = ck.shape
    row = lambda w: pl.BlockSpec((1, P, w), lambda b: (b, 0, 0))
    widths = dict(kfa=N_PAIRS * PAIR_W, vf=FOX_W, kc=N_PAIRS * PAIR_W, vm=MLA_HEADS * MLA_V)
    in_w = (ck.shape[2], cv.shape[2], LANES, cckv.shape[2], LANES)
    block_bytes = P * (sum(in_w) * 4 + sum(widths.values()) * 2)
    outs = pl.pallas_call(
        functools.partial(_prefix_kernel, cs=cs),
        grid=(B,),
        in_specs=[row(w) for w in in_w] + [_const_spec(w_ukv.shape), _const_spec(tri.shape)],
        out_specs=[row(w) for w in widths.values()],
        out_shape=[jax.ShapeDtypeStruct((B, P, w), BF16) for w in widths.values()],
        compiler_params=pltpu.CompilerParams(
            dimension_semantics=("arbitrary",),
            vmem_limit_bytes=_vmem_limit(block_bytes, w_ukv.size * 2 + tri.size * 2, P * 1024 * 4 * 3)),
        name="prefix",
    )(ck, cv, lf_exp, cckv, kr4, w_ukv, tri)
    return dict(zip(widths.keys(), outs))


def _online_step(state, qa, k, v, mask=None, bias=None):
    m, l, acc = state
    s = lax.dot_general(qa, k, (((1,), (1,)), ((), ())), preferred_element_type=F32)
    if mask is not None:
        s = jnp.where(mask, s, NEG)
    if bias is not None:
        s = s + bias
    m_new = jnp.maximum(m, jnp.max(s, axis=1, keepdims=True))
    alpha = jnp.exp(m - m_new)
    p = jnp.exp(s - m_new)
    l = alpha * l + jnp.sum(p, axis=1, keepdims=True)
    acc = alpha * acc + jnp.dot(p.astype(BF16), v, preferred_element_type=F32)
    return m_new, l, acc


def _attend_pair(q_ops, k_ref, v_ref, pk_ref, pv_ref, diag_mask, pref_bias, *, tq, tk, tkp):
    i = pl.program_id(2)
    init = (jnp.full((tq, 1), NEG, F32), jnp.zeros((tq, 1), F32), jnp.zeros((tq, LANES), F32))
    states = [init for _ in q_ops]
    if pk_ref is not None:
        for pt in range(pk_ref.shape[1] // tkp):
            k = pk_ref[0, pt * tkp:(pt + 1) * tkp, :]
            v = pv_ref[0, pt * tkp:(pt + 1) * tkp, :]
            states = [_online_step(st, qa, k, v, bias=pref_bias) for st, qa in zip(states, q_ops)]

    def body(t, sts):
        start = pl.multiple_of(t * tk, tk)
        k = k_ref[0, pl.ds(start, tk), :]
        v = v_ref[0, pl.ds(start, tk), :]
        return tuple(_online_step(st, qa, k, v) for st, qa in zip(sts, q_ops))

    states = lax.fori_loop(0, i * (tq // tk), body, tuple(states))
    start = pl.multiple_of(i * tq, tq)
    k = k_ref[0, pl.ds(start, tq), :]
    v = v_ref[0, pl.ds(start, tq), :]
    outs = []
    for st, qa in zip(states, q_ops):
        _, l, acc = _online_step(st, qa, k, v, mask=diag_mask)
        outs.append(acc / l)
    lane = lax.broadcasted_iota(jnp.int32, (tq, LANES), 1)
    return jnp.where(lane < FOX_HEAD_DIM, outs[0], outs[1])


def _attn_kernel(*refs, tq, tk, tkp, has_prefix, n_pref):
    if has_prefix:
        (qf_ref, qn_ref, qr_ref, kfa_ref, vf_ref, kc_ref, vm_ref,
         pkfa_ref, pvf_ref, pkc_ref, pvm_ref, of_ref, om_ref) = refs
    else:
        qf_ref, qn_ref, qr_ref, kfa_ref, vf_ref, kc_ref, vm_ref, of_ref, om_ref = refs
        pkfa_ref = pvf_ref = pkc_ref = pvm_ref = None
    b, j, i = pl.program_id(0), pl.program_id(1), pl.program_id(2)
    lane = lax.broadcasted_iota(jnp.int32, (tq, LANES), 1)
    row = lax.broadcasted_iota(jnp.int32, (tq, tq), 0) + i * tq
    col = lax.broadcasted_iota(jnp.int32, (tq, tq), 1) + i * tq
    pref_bias = None if (not has_prefix or n_pref is None) else jnp.where(b < n_pref, 0.0, NEG)

    qf, qn, qr = qf_ref[0], qn_ref[0], qr_ref[0]
    zero = jnp.zeros_like(qf)
    rope_groups = LANES // MLA_ROPE
    fox_ops, mla_ops = [], []
    for s in range(HEADS_PER_PAIR):
        h = HEADS_PER_PAIR * j + s
        own = (lane >= s * FOX_HEAD_DIM) & (lane < (s + 1) * FOX_HEAD_DIM)
        minus_f = jnp.where((lane < F_PARTS * FOX_HEADS) & ((lane & (FOX_HEADS - 1)) == h), -1.0, 0.0).astype(BF16)
        fox_ops.append(jnp.concatenate([jnp.where(own, qf, zero), minus_f], axis=1))
        rope_own = (lane >> int(math.log2(MLA_ROPE))) == (h & (rope_groups - 1))
        mla_ops.append(jnp.concatenate([jnp.where(own, qn, zero), jnp.where(rope_own, qr, zero)], axis=1))

    kw = dict(tq=tq, tk=tk, tkp=tkp)
    chunk_shift = int(math.log2(CHUNK))
    of_ref[0] = _attend_pair(fox_ops, kfa_ref, vf_ref, pkfa_ref, pvf_ref, col <= row, pref_bias, **kw).astype(BF16)
    om_ref[0] = _attend_pair(mla_ops, kc_ref, vm_ref, pkc_ref, pvm_ref,
                             (col >> chunk_shift) <= (row >> chunk_shift), pref_bias, **kw).astype(BF16)


def _attention(ops, prefix, *, tq, tk, tkp, n_pref=None):
    B, n, _ = ops['qf'].shape
    assert n % tq == 0 and tq % tk == 0
    nq = n // tq
    rope_cols_per_pair = LANES // (HEADS_PER_PAIR * MLA_ROPE)
    qspec = pl.BlockSpec((1, tq, LANES), lambda b, j, i: (b, i, j))
    qrspec = pl.BlockSpec((1, tq, LANES), lambda b, j, i: (b, i, j // rope_cols_per_pair))
    kspec = pl.BlockSpec((1, n, PAIR_W), lambda b, j, i: (b, 0, j))
    vspec = pl.BlockSpec((1, n, LANES), lambda b, j, i: (b, 0, j))
    args = [ops['qf'], ops['qn'], ops['qr'], ops['kfa'], ops['vf'], ops['kc'], ops['vm']]
    in_specs = [qspec, qspec, qrspec, kspec, vspec, kspec, vspec]
    block_bytes = 2 * (3 * tq * LANES + 2 * n * (PAIR_W + LANES) + 2 * tq * LANES)
    if prefix is not None:
        Bp, P, _ = prefix['kfa'].shape
        assert P % tkp == 0
        if Bp == 1:
            pb = lambda b: 0
        elif n_pref is not None:
            pb = lambda b: jnp.minimum(b, n_pref - 1)
        else:
            pb = lambda b: b
        pk = pl.BlockSpec((1, P, PAIR_W), lambda b, j, i: (pb(b), 0, j))
        pv = pl.BlockSpec((1, P, LANES), lambda b, j, i: (pb(b), 0, j))
        args += [prefix['kfa'], prefix['vf'], prefix['kc'], prefix['vm']]
        in_specs += [pk, pv, pk, pv]
        block_bytes += 2 * 2 * P * (PAIR_W + LANES)
    ospec = pl.BlockSpec((1, tq, LANES), lambda b, j, i: (b, i, j))
    temp_bytes = 12 * tq * max(tk, tkp if prefix is not None else tk) * 4 + 16 * tq * LANES * 4
    of, om = pl.pallas_call(
        functools.partial(_attn_kernel, tq=tq, tk=tk, tkp=tkp, has_prefix=prefix is not None, n_pref=n_pref),
        grid=(B, N_PAIRS, nq), in_specs=in_specs, out_specs=[ospec, ospec],
        out_shape=[jax.ShapeDtypeStruct((B, n, FOX_W), BF16), jax.ShapeDtypeStruct((B, n, MLA_HEADS * MLA_V), BF16)],
        compiler_params=pltpu.CompilerParams(
            dimension_semantics=("arbitrary", "arbitrary", "arbitrary"),
            vmem_limit_bytes=_vmem_limit(block_bytes, 0, temp_bytes)),
        name="attn",
    )(*args)
    return of, om


def _ffn_kernel(*refs, d_ff, fc, flat_group):
    if flat_group:
        (h_ref, of_ref, om_ref, g_ref, left_ref, wof_ref, wom_ref, wout_ref, gffn_ref, wup_ref,
         cw_ref, cb_ref, wdown_ref, gfin_ref, y_ref, u_ref, l1_ref, l2_ref) = refs
        l1_ref[...] = jnp.zeros_like(l1_ref)
        l2_ref[...] = jnp.zeros_like(l2_ref)
        for g in range(left_ref.shape[0]):
            r0 = g * flat_group
            l1_ref[r0:r0 + 1, :] = left_ref[g, 1:2, :]
            l2_ref[r0:r0 + 1, :] = left_ref[g, 0:1, :]
            l2_ref[r0 + 1:r0 + 2, :] = left_ref[g, 1:2, :]
    else:
        (h_ref, of_ref, om_ref, g_ref, left_ref, wof_ref, wom_ref, wout_ref, gffn_ref, wup_ref,
         cw_ref, cb_ref, wdown_ref, gfin_ref, y_ref, u_ref, carry_ref) = refs

        @pl.when(pl.program_id(1) == 0)
        def _():
            carry_ref[0:CONV_W - 1, :] = left_ref[0]

    ts, d = h_ref.shape[1], h_ref.shape[2]
    ya = jnp.dot(of_ref[0], wof_ref[...], preferred_element_type=F32)
    yb = jnp.dot(om_ref[0], wom_ref[...], preferred_element_type=F32)
    mix = g_ref[0, :, :d] * ya + g_ref[0, :, d:] * yb
    h1 = h_ref[0] + jnp.dot(mix.astype(BF16), wout_ref[...], preferred_element_type=F32)
    xn = _rms(h1, gffn_ref[...]).astype(BF16)

    row = lax.broadcasted_iota(jnp.int32, (ts, 1), 0)
    if flat_group:
        keep1 = ((row & (flat_group - 1)) >= 1).astype(F32)
        keep2 = ((row & (flat_group - 1)) >= 2).astype(F32)

    def conv(c0):
        cols = slice(c0, c0 + fc)
        u = jnp.dot(xn, wup_ref[:, cols], preferred_element_type=F32)
        u1 = pltpu.roll(u, 1, 0)
        u2 = pltpu.roll(u, 2, 0)
        if flat_group:
            u_ref[0, :, cols] = u
            u1 = u1 * keep1 + l1_ref[:, cols]
            u2 = u2 * keep2 + l2_ref[:, cols]
        else:
            prev2 = carry_ref[0:1, cols]
            prev1 = carry_ref[1:2, cols]
            u1 = jnp.where(row == 0, prev1, u1)
            u2 = jnp.where(row == 0, prev2, jnp.where(row == 1, prev1, u2))
            carry_ref[0:CONV_W - 1, cols] = u[ts - (CONV_W - 1):, :]
            u_ref[0, :, cols] = u[ts - (CONV_W - 1):, :]
        return cb_ref[:, cols] + u2 * cw_ref[0:1, cols] + u1 * cw_ref[1:2, cols] + u * cw_ref[2:3, cols]

    acc = jnp.zeros((ts, d), F32)
    for c in range(d_ff // fc):
        gate = conv(c * fc)
        val = conv(d_ff + c * fc)
        act = (jax.nn.silu(gate) * val).astype(BF16)
        acc = acc + jnp.dot(act, wdown_ref[c * fc:(c + 1) * fc, :], preferred_element_type=F32)
    y_ref[0] = _rms(h1 + acc, gfin_ref[...])


def _merge_ffn(h, of, om, gates, left, wts, *, ts, flat_group=0):
    B, n, d = h.shape
    d_ff = wts['w_down'].shape[0]
    fc = MXU_DIM
    assert n % ts == 0 and d_ff % fc == 0
    row = lambda w: pl.BlockSpec((1, ts, w), lambda b, t: (b, t, 0))
    consts = [wts[k] for k in ('w_o_fox', 'w_o_mla', 'w_out', 'g_ffn', 'w_up', 'conv_w', 'conv_b', 'w_down', 'g_fin')]
    const_specs = [_const_spec(c.shape) for c in consts]
    in_specs = [row(d), row(of.shape[2]), row(om.shape[2]), row(2 * d)]
    args = [h, of, om, gates, left]
    if flat_group:
        assert B == 1 and ts % flat_group == 0 and left.shape[0] * flat_group == n
        gpt = ts // flat_group
        in_specs.append(pl.BlockSpec((gpt, CONV_W - 1, 2 * d_ff), lambda b, t: (t, 0, 0)))
        u_shape, u_spec = (1, n, 2 * d_ff), row(2 * d_ff)
        scratch = [pltpu.VMEM((ts, 2 * d_ff), F32), pltpu.VMEM((ts, 2 * d_ff), F32)]
        const_extra = 2 * ts * 2 * d_ff * 4 + 2 * gpt * 8 * 2 * d_ff * 4
    else:
        bl = left.shape[0]
        in_specs.append(pl.BlockSpec((1, CONV_W - 1, 2 * d_ff), lambda b, t: (b if bl > 1 else 0, 0, 0)))
        u_shape = (B, CONV_W - 1, 2 * d_ff)
        u_spec = pl.BlockSpec((1, CONV_W - 1, 2 * d_ff), lambda b, t: (b, 0, 0))
        scratch = [pltpu.VMEM((8, 2 * d_ff), F32)]
        const_extra = 8 * 2 * d_ff * 4
    u_rows = ts if flat_group else 8
    block_bytes = ts * (d * 4 + (of.shape[2] + om.shape[2]) * 2 + 2 * d * 4 + d * 4) + u_rows * 2 * d_ff * 4
    const_bytes = sum(c.size * c.dtype.itemsize for c in consts) + const_extra
    temp_bytes = ts * d * 4 * 8 + ts * fc * 4 * 16
    y, u = pl.pallas_call(
        functools.partial(_ffn_kernel, d_ff=d_ff, fc=fc, flat_group=flat_group),
        grid=(B, n // ts), in_specs=in_specs + const_specs,
        out_specs=[row(d), u_spec],
        out_shape=[jax.ShapeDtypeStruct((B, n, d), F32), jax.ShapeDtypeStruct(u_shape, F32)],
        scratch_shapes=scratch,
        compiler_params=pltpu.CompilerParams(
            dimension_semantics=("arbitrary", "arbitrary"),
            vmem_limit_bytes=_vmem_limit(block_bytes, const_bytes, temp_bytes)),
        name="ffn",
    )(*args, *consts)
    return y, u


def _rope_tables(pos):
    half = MLA_ROPE // 2
    inv = ROPE_BASE ** (-jnp.arange(0, MLA_ROPE, 2, dtype=F32) / MLA_ROPE)
    ang = pos.astype(F32)[:, None] * inv[None, :]
    cos, sin = jnp.cos(ang), jnp.sin(ang)
    zero = jnp.zeros_like(sin)
    reps = LANES // MLA_ROPE
    ct = jnp.tile(jnp.concatenate([cos, cos], axis=1), (1, reps))
    sa = jnp.tile(jnp.concatenate([-sin, zero], axis=1), (1, reps))
    sb = jnp.tile(jnp.concatenate([zero, sin], axis=1), (1, reps))
    assert half * 2 == MLA_ROPE
    return ct, sa, sb


def _lower_tri(n, group):
    i = np.arange(n)
    m = (i[None, :] <= i[:, None]) & ((i[None, :] // group) == (i[:, None] // group))
    return jnp.asarray(m, BF16)


def _prepare_weights(norm_mix_g, w_in, b_forget, mla_q_norm_g, w_uq, mla_kv_norm_g, w_ukv, w_o_fox, w_o_mla, w_out,
                     norm_ffn_g, w_up, conv_w, conv_b, w_down, norm_final_g):
    d = w_in.shape[0]
    q_lora, kv_lora = mla_q_norm_g.shape[0], mla_kv_norm_g.shape[0]
    off_ff = 3 * FOX_W
    off_cq = off_ff + FOX_HEADS
    off_ckv = off_cq + q_lora
    off_kr = off_ckv + kv_lora
    off_gate = off_kr + MLA_ROPE
    pad = LANES - F_PARTS * FOX_HEADS
    w_ff = jnp.pad(jnp.tile(w_in[:, off_ff:off_cq], (1, F_PARTS)), ((0, 0), (0, pad)))
    w_a = jnp.concatenate([
        w_in[:, :FOX_W] * FOX_SCALE,
        w_in[:, FOX_W:off_ff], w_ff, w_in[:, off_cq:off_kr],
        jnp.tile(w_in[:, off_kr:off_gate], (1, LANES // MLA_ROPE)),
        w_in[:, off_gate:]], axis=1).astype(BF16)
    uq = w_uq.reshape(q_lora, MLA_HEADS, MLA_QK)
    w_uq_p = jnp.concatenate([uq[:, :, :MLA_NOPE].reshape(q_lora, -1), uq[:, :, MLA_NOPE:].reshape(q_lora, -1)], axis=1)
    ukv = w_ukv.reshape(kv_lora, MLA_HEADS, MLA_NOPE + MLA_V)
    w_ukv_p = jnp.concatenate([ukv[:, :, :MLA_NOPE].reshape(kv_lora, -1), ukv[:, :, MLA_NOPE:].reshape(kv_lora, -1)], axis=1)
    return dict(
        g_mix=norm_mix_g.reshape(1, d), w_a=w_a,
        b_exp=jnp.pad(jnp.tile(b_forget, F_PARTS), (0, pad)).reshape(1, LANES),
        g_q=mla_q_norm_g.reshape(1, q_lora), w_uq=w_uq_p.astype(BF16),
        g_kv=mla_kv_norm_g.reshape(1, kv_lora), w_ukv=w_ukv_p.astype(BF16),
        w_o_fox=w_o_fox.astype(BF16), w_o_mla=w_o_mla.astype(BF16), w_out=w_out.astype(BF16),
        g_ffn=norm_ffn_g.reshape(1, d), w_up=w_up.astype(BF16), conv_w=conv_w, conv_b=conv_b.reshape(1, -1),
        w_down=w_down.astype(BF16), g_fin=norm_final_g.reshape(1, d))


PROMPT_ROWS = 256
PROMPT_TQ = 256
SMALL_ROWS = LANES


def kernel(x_prompt, x_sample, cache_fox_k, cache_fox_v, cache_fox_logf, cache_mla_ckv, cache_mla_krope, state_ffn_conv, meta_tokens, norm_mix_g, w_in, b_forget, mla_q_norm_g, w_uq, mla_kv_norm_g, w_ukv, w_o_fox, w_o_mla, w_out, norm_ffn_g, w_up, conv_w, conv_b, w_down, norm_final_g):
    assert w_in.shape[0] == 1, "single-layer model"
    B, S, d = x_prompt.shape
    Bs, ns, _ = x_sample.shape
    P = cache_fox_k.shape[2]
    n_meta = meta_tokens.shape[0]
    assert n_meta == N_META == ns
    wts = _prepare_weights(norm_mix_g[0], w_in[0], b_forget[0], mla_q_norm_g[0], w_uq[0], mla_kv_norm_g[0], w_ukv[0],
                           w_o_fox[0], w_o_mla[0], w_out[0], norm_ffn_g[0], w_up[0], conv_w[0], conv_b[0], w_down[0],
                           norm_final_g)
    d_ff2 = w_up.shape[2]

    r_pad = -(-(Bs + 1) * ns // SMALL_ROWS) * SMALL_ROWS
    n_groups = r_pad // ns
    n_fill = r_pad - (Bs + 1) * ns
    rows = jnp.concatenate([x_sample.reshape(Bs * ns, d), jnp.zeros((n_fill, d), x_sample.dtype),
                            meta_tokens.astype(x_sample.dtype)], axis=0)[None]
    pos_small = jnp.concatenate([jnp.tile(P + jnp.arange(ns), Bs), jnp.zeros((n_fill,), jnp.int32), jnp.arange(n_meta)])
    zero_f = jnp.zeros((1, 8, LANES), F32)
    sm = _project(rows, _rope_tables(pos_small), zero_f, wts, _lower_tri(r_pad, ns), ts=r_pad, cs=r_pad)
    sm_b = {k: v.reshape(n_groups, ns, v.shape[-1]) for k, v in sm.items() if k != 'ftot'}
    live = lambda a: jnp.concatenate([a[:Bs], a[n_groups - 1:]], axis=0)

    lf_exp = jnp.pad(jnp.tile(cache_fox_logf[0], (1, 1, F_PARTS)), ((0, 0), (0, 0), (0, LANES - F_PARTS * FOX_HEADS)))
    prefix_s = _prefix_operands(
        cache_fox_k[0].reshape(Bs, P, FOX_W), cache_fox_v[0].reshape(Bs, P, FOX_W), lf_exp, cache_mla_ckv[0],
        jnp.tile(cache_mla_krope[0], (1, 1, LANES // MLA_ROPE)), wts['w_ukv'], _lower_tri(MXU_DIM, MXU_DIM), cs=MXU_DIM)
    of_s, om_s = _attention({k: live(v) for k, v in sm_b.items()}, prefix_s, tq=ns, tk=ns, tkp=P, n_pref=Bs)

    def spread(a):
        return jnp.concatenate([a[:Bs].reshape(Bs * ns, -1), jnp.zeros((n_fill, a.shape[-1]), a.dtype), a[Bs]], axis=0)[None]

    left_groups = jnp.concatenate([state_ffn_conv[0], jnp.zeros((n_groups - Bs, CONV_W - 1, d_ff2), F32)], axis=0)
    y_s, u_s = _merge_ffn(rows, spread(of_s), spread(om_s), sm['gate'], left_groups, wts, ts=SMALL_ROWS, flat_group=ns)
    y_sample = y_s[0, :Bs * ns].reshape(Bs, ns, d)
    u_groups = u_s.reshape(n_groups, ns, d_ff2)
    conv_s = u_groups[:Bs, ns - (CONV_W - 1):]
    left_meta = u_groups[n_groups - 1:, ns - (CONV_W - 1):]

    meta_ops = {k: sm_b[k][n_groups - 1:] for k in ('kfa', 'vf', 'kc', 'vm')}
    pos_p = n_meta + jnp.arange(S)
    pr = _project(x_prompt, _rope_tables(pos_p), sm['ftot'], wts, _lower_tri(MXU_DIM, MXU_DIM), ts=PROMPT_ROWS, cs=MXU_DIM)
    of_p, om_p = _attention(pr, meta_ops, tq=PROMPT_TQ, tk=PROMPT_TQ, tkp=n_meta)
    y_prompt, conv_p = _merge_ffn(x_prompt, of_p, om_p, pr['gate'], left_meta, wts, ts=PROMPT_ROWS)

    def with_meta(name, tail):
        meta = jnp.broadcast_to(sm_b[name][n_groups - 1:], (B, n_meta, sm_b[name].shape[-1]))
        return jnp.concatenate([meta, pr[name]], axis=1).reshape((1, B, n_meta + S) + tail)

    def sample_rows(name, tail):
        return sm_b[name][:Bs].reshape((1, Bs, ns) + tail)

    hd = (FOX_HEADS, FOX_HEAD_DIM)
    return (y_prompt, y_sample,
            with_meta('fk', hd), with_meta('fv', hd), with_meta('logf', (FOX_HEADS,)),
            with_meta('ckv', (cache_mla_ckv.shape[-1],)), with_meta('kr', (MLA_ROPE,)), conv_p[None],
            sample_rows('fk', hd), sample_rows('fv', hd), sample_rows('logf', (FOX_HEADS,)),
            sample_rows('ckv', (cache_mla_ckv.shape[-1],)), sample_rows('kr', (MLA_ROPE,)), conv_s[None])
```

```python
import functools
import math

import jax
import jax.numpy as jnp
import numpy as np
from jax import lax
from jax.experimental import pallas as pl
from jax.experimental.pallas import tpu as pltpu

F32 = jnp.float32
BF16 = jnp.bfloat16

N_META = 16
CHUNK = 64
EPS = 1e-6
NEG = -1e30
FOX_HEADS = 8
FOX_HEAD_DIM = 64
MLA_HEADS = 8
MLA_NOPE = 64
MLA_ROPE = 32
MLA_V = 64
MLA_QK = MLA_NOPE + MLA_ROPE
ROPE_BASE = 10000.0
CONV_W = 3

FOX_W = FOX_HEADS * FOX_HEAD_DIM
FOX_SCALE = 1.0 / math.sqrt(FOX_HEAD_DIM)
MLA_SCALE = 1.0 / math.sqrt(MLA_QK)

LANES = 128
MXU_DIM = 256
VMEM_BYTES_V7X = 64 * 1024 * 1024
F_PARTS = 3
HEADS_PER_PAIR = LANES // FOX_HEAD_DIM
N_PAIRS = FOX_HEADS // HEADS_PER_PAIR
PAIR_W = 2 * LANES
N_CHAINS = 2 * HEADS_PER_PAIR

C_FQ = 0
C_FK = C_FQ + FOX_W
C_FV = C_FK + FOX_W
C_FF = C_FV + FOX_W
C_CQ = C_FF + LANES


def _rms(x, g):
    return x * lax.rsqrt(jnp.mean(x * x, axis=-1, keepdims=True) + EPS) * g


def _split3(x):
    hi = x.astype(BF16)
    r = x - hi.astype(F32)
    mid = r.astype(BF16)
    lo = (r - mid.astype(F32)).astype(BF16)
    return hi, mid, lo


def _cumsum_rows(lf, tri, carry, cs):
    hi, mid, lo = _split3(lf)
    pieces = []
    for c in range(lf.shape[0] // cs):
        sl = slice(c * cs, (c + 1) * cs)
        fc = (jnp.dot(tri, hi[sl], preferred_element_type=F32)
              + jnp.dot(tri, mid[sl], preferred_element_type=F32)
              + jnp.dot(tri, lo[sl], preferred_element_type=F32)) + carry
        carry = fc[cs - 1:cs, :]
        pieces.append(fc)
    f = pieces[0] if len(pieces) == 1 else jnp.concatenate(pieces, axis=0)
    return f, carry


def _forget_columns(f):
    hi, mid, lo = _split3(f)
    lane = lax.broadcasted_iota(jnp.int32, f.shape, 1)
    zero = jnp.zeros_like(hi)
    return jnp.where(lane < FOX_HEADS, hi,
                     jnp.where(lane < 2 * FOX_HEADS, mid,
                               jnp.where(lane < F_PARTS * FOX_HEADS, lo, zero)))


def _store_pairs(ref, main, extra):
    for j in range(N_PAIRS):
        ref[0, :, PAIR_W * j:PAIR_W * j + LANES] = main[:, LANES * j:LANES * (j + 1)]
        ref[0, :, PAIR_W * j + LANES:PAIR_W * (j + 1)] = extra


def _rope(x, ct, sa, sb):
    return x * ct + pltpu.roll(x, LANES - MLA_ROPE // 2, 1) * sa + pltpu.roll(x, MLA_ROPE // 2, 1) * sb


def _proj_kernel(x_ref, ct_ref, sa_ref, sb_ref, finit_ref, gmix_ref, w_ref, bexp_ref, gq_ref, wuq_ref,
                 gkv_ref, wukv_ref, tri_ref,
                 fk_ref, fv_ref, logf_ref, ckv_ref, kr_ref, qf_ref, kfa_ref, vf_ref, qn_ref, qr_ref,
                 kc_ref, vm_ref, gate_ref, ftot_ref, carry_ref, *, cs, q_lora, kv_lora, d_model):
    t = pl.program_id(1)

    @pl.when(t == 0)
    def _():
        carry_ref[...] = finit_ref[0]

    c_ckv = C_CQ + q_lora
    c_kr = c_ckv + kv_lora
    c_gate = c_kr + LANES
    xb = _rms(x_ref[0], gmix_ref[...]).astype(BF16)

    def proj(a, b):
        return jnp.dot(xb, w_ref[:, a:b], preferred_element_type=F32)

    qf_ref[0] = proj(C_FQ, C_FK).T.astype(BF16)
    fk = proj(C_FK, C_FV)
    fk_ref[0] = fk
    fv = proj(C_FV, C_FF)
    fv_ref[0] = fv
    vf_ref[0] = fv.T.astype(BF16)
    lf = jax.nn.log_sigmoid(proj(C_FF, C_CQ) + bexp_ref[...])
    logf_ref[0] = lf[:, :FOX_HEADS]
    f, carry = _cumsum_rows(lf, tri_ref[...], carry_ref[0:1, :], cs)
    carry_ref[0:1, :] = carry
    ftot_ref[0] = jnp.broadcast_to(carry, ftot_ref.shape[1:])
    _store_pairs(kfa_ref, fk.astype(BF16), _forget_columns(f))

    ct, sa, sb = ct_ref[...], sa_ref[...], sb_ref[...]
    cqn = _rms(proj(C_CQ, c_ckv), gq_ref[...]).astype(BF16)
    q = jnp.dot(cqn, wuq_ref[...], preferred_element_type=F32) * MLA_SCALE
    nope_w = MLA_HEADS * MLA_NOPE
    qn_ref[0] = q[:, :nope_w].T.astype(BF16)
    for c in range(MLA_HEADS * MLA_ROPE // LANES):
        qr_ref[0, LANES * c:LANES * (c + 1), :] = _rope(
            q[:, nope_w + LANES * c:nope_w + LANES * (c + 1)], ct, sa, sb).T.astype(BF16)
    ckvn = _rms(proj(c_ckv, c_kr), gkv_ref[...])
    ckv_ref[0] = ckvn
    kv = jnp.dot(ckvn.astype(BF16), wukv_ref[...], preferred_element_type=F32)
    vm_ref[0] = kv[:, nope_w:].T.astype(BF16)
    kr4 = _rope(proj(c_kr, c_gate), ct, sa, sb)
    kr_ref[0] = kr4[:, :MLA_ROPE]
    _store_pairs(kc_ref, kv[:, :nope_w].astype(BF16), kr4.astype(BF16))

    gate_ref[0] = jax.nn.sigmoid(proj(c_gate, c_gate + 2 * d_model))


TRANSPOSED = ('qf', 'vf', 'qn', 'qr', 'vm')


def _const_spec(shape):
    return pl.BlockSpec(shape, lambda *_: (0,) * len(shape), pipeline_mode=pl.Buffered(1))


def _vmem_limit(block_bytes, const_bytes, temp_bytes):
    need = 2 * block_bytes + const_bytes + temp_bytes
    assert need < VMEM_BYTES_V7X, need
    return int(need)


def _project(x, tables, finit, wts, tri, *, ts, cs):
    B, n, d = x.shape
    assert n % ts == 0 and ts % cs == 0
    nt = n // ts
    q_lora = wts['g_q'].shape[1]
    kv_lora = wts['g_kv'].shape[1]
    ct, sa, sb = tables
    row = lambda w: pl.BlockSpec((1, ts, w), lambda b, t: (b, t, 0))
    tab = pl.BlockSpec((ts, LANES), lambda b, t: (t, 0))
    in_specs = [row(d), tab, tab, tab,
                pl.BlockSpec((1, 8, LANES), lambda b, t: (0, 0, 0)),
                _const_spec(wts['g_mix'].shape), _const_spec(wts['w_a'].shape), _const_spec(wts['b_exp'].shape),
                _const_spec(wts['g_q'].shape), _const_spec(wts['w_uq'].shape), _const_spec(wts['g_kv'].shape),
                _const_spec(wts['w_ukv'].shape), _const_spec(tri.shape)]
    widths = dict(fk=(FOX_W, F32), fv=(FOX_W, F32), logf=(FOX_HEADS, F32), ckv=(kv_lora, F32), kr=(MLA_ROPE, F32),
                  qf=(FOX_W, BF16), kfa=(N_PAIRS * PAIR_W, BF16), vf=(FOX_W, BF16),
                  qn=(MLA_HEADS * MLA_NOPE, BF16), qr=(MLA_HEADS * MLA_ROPE, BF16),
                  kc=(N_PAIRS * PAIR_W, BF16), vm=(MLA_HEADS * MLA_V, BF16), gate=(2 * d, F32))
    col = lambda w: pl.BlockSpec((1, w, ts), lambda b, t: (b, 0, t))
    out_shape = [jax.ShapeDtypeStruct((B, w, n) if k in TRANSPOSED else (B, n, w), dt) for k, (w, dt) in widths.items()]
    out_specs = [col(w) if k in TRANSPOSED else row(w) for k, (w, _) in widths.items()]
    out_shape.append(jax.ShapeDtypeStruct((B, 8, LANES), F32))
    out_specs.append(pl.BlockSpec((1, 8, LANES), lambda b, t: (b, 0, 0)))
    block_bytes = ts * (d * 4 + 3 * LANES * 4 + sum(w * jnp.dtype(dt).itemsize for w, dt in widths.values()))
    const_bytes = sum(int(np.prod(wts[k].shape)) * wts[k].dtype.itemsize for k in ('w_a', 'w_uq', 'w_ukv')) + tri.size * 2
    temp_bytes = ts * (2 * d + 2 * d) * 4 * 2
    outs = pl.pallas_call(
        functools.partial(_proj_kernel, cs=cs, q_lora=q_lora, kv_lora=kv_lora, d_model=d),
        grid=(B, nt), in_specs=in_specs, out_specs=out_specs, out_shape=out_shape,
        scratch_shapes=[pltpu.VMEM((8, LANES), F32)],
        compiler_params=pltpu.CompilerParams(
            dimension_semantics=("arbitrary", "arbitrary"),
            vmem_limit_bytes=_vmem_limit(block_bytes, const_bytes, temp_bytes)),
        name="proj",
    )(x, ct, sa, sb, finit, wts['g_mix'], wts['w_a'], wts['b_exp'], wts['g_q'], wts['w_uq'], wts['g_kv'],
      wts['w_ukv'], tri)
    res = dict(zip(widths.keys(), outs[:-1]))
    res['ftot'] = outs[-1]
    return res


def _prefix_kernel(ck_ref, cv_ref, lf_ref, ckv_ref, kr4_ref, wukv_ref, tri_ref,
                   kfa_ref, vf_ref, kc_ref, vm_ref, *, cs):
    f, total = _cumsum_rows(lf_ref[0], tri_ref[...], jnp.zeros((1, LANES), F32), cs)
    _store_pairs(kfa_ref, ck_ref[0].astype(BF16), _forget_columns(f - total))
    vf_ref[0] = cv_ref[0].T.astype(BF16)
    kv = jnp.dot(ckv_ref[0].astype(BF16), wukv_ref[...], preferred_element_type=F32)
    nope_w = MLA_HEADS * MLA_NOPE
    vm_ref[0] = kv[:, nope_w:].T.astype(BF16)
    _store_pairs(kc_ref, kv[:, :nope_w].astype(BF16), kr4_ref[0].astype(BF16))


def _prefix_operands(ck, cv, lf_exp, cckv, kr4, w_ukv, tri, *, cs):
    B, P, _ = ck.shape
    row = lambda w: pl.BlockSpec((1, P, w), lambda b: (b, 0, 0))
    col = lambda w: pl.BlockSpec((1, w, P), lambda b: (b, 0, 0))
    widths = dict(kfa=N_PAIRS * PAIR_W, vf=FOX_W, kc=N_PAIRS * PAIR_W, vm=MLA_HEADS * MLA_V)
    in_w = (ck.shape[2], cv.shape[2], LANES, cckv.shape[2], LANES)
    block_bytes = P * (sum(in_w) * 4 + sum(widths.values()) * 2)
    outs = pl.pallas_call(
        functools.partial(_prefix_kernel, cs=cs),
        grid=(B,),
        in_specs=[row(w) for w in in_w] + [_const_spec(w_ukv.shape), _const_spec(tri.shape)],
        out_specs=[col(w) if k in TRANSPOSED else row(w) for k, w in widths.items()],
        out_shape=[jax.ShapeDtypeStruct((B, w, P) if k in TRANSPOSED else (B, P, w), BF16) for k, w in widths.items()],
        compiler_params=pltpu.CompilerParams(
            dimension_semantics=("arbitrary",),
            vmem_limit_bytes=_vmem_limit(block_bytes, w_ukv.size * 2 + tri.size * 2, P * 1024 * 4 * 3)),
        name="prefix",
    )(ck, cv, lf_exp, cckv, kr4, w_ukv, tri)
    return dict(zip(widths.keys(), outs))


def _scores(ks, qts):
    return tuple(jnp.dot(k, qt, preferred_element_type=F32) for k, qt in zip(ks, qts))


def _online_update(states, scores, vts, masks=None, bias=None):
    mids = []
    for c, ((m, l, acc), s) in enumerate(zip(states, scores)):
        if masks is not None:
            s = jnp.where(masks[c], s, NEG)
        if bias is not None:
            s = s + bias
        m_new = jnp.maximum(m, jnp.max(s, axis=0, keepdims=True))
        alpha = jnp.exp(m - m_new)
        p = jnp.exp(s - m_new)
        mids.append((m_new, alpha * l + jnp.sum(p, axis=0, keepdims=True), alpha * acc, p.astype(BF16)))
    return tuple((m_new, l, acc + jnp.dot(vt, p, preferred_element_type=F32))
                 for (m_new, l, acc, p), vt in zip(mids, vts))


def _attn_kernel(*refs, tq, tk, tkp, nq, has_prefix, n_pref, transpose_out):
    if has_prefix:
        (qf_ref, qn_ref, qr_ref, kfa_ref, vf_ref, kc_ref, vm_ref,
         pkfa_ref, pvf_ref, pkc_ref, pvm_ref, of_ref, om_ref, sa_ref, sb_ref) = refs
    else:
        qf_ref, qn_ref, qr_ref, kfa_ref, vf_ref, kc_ref, vm_ref, of_ref, om_ref, sa_ref, sb_ref = refs
    b, j, i = pl.program_id(0), pl.program_id(1), pl.program_id(2)
    pref_bias = None if (not has_prefix or n_pref is None) else jnp.where(b < n_pref, 0.0, NEG)

    qf, qn, qr = qf_ref[0], qn_ref[0], qr_ref[0]
    frow = lax.broadcasted_iota(jnp.int32, (LANES, tq), 0)
    zero_half = jnp.zeros((FOX_HEAD_DIM, tq), BF16)
    rope_groups = LANES // MLA_ROPE
    chains = []
    for mixer in range(2):
        for s in range(HEADS_PER_PAIR):
            h = HEADS_PER_PAIR * j + s
            vrows = slice(s * FOX_HEAD_DIM, (s + 1) * FOX_HEAD_DIM)
            q_all = qf if mixer == 0 else qn
            own = [zero_half] * HEADS_PER_PAIR
            own[s] = q_all[vrows, :]
            if mixer == 0:
                extra = jnp.where((frow < F_PARTS * FOX_HEADS) & ((frow & (FOX_HEADS - 1)) == h), -1.0, 0.0).astype(BF16)
                refs_m = (kfa_ref, vf_ref) + ((pkfa_ref, pvf_ref) if has_prefix else (None, None))
            else:
                extra = jnp.where((frow >> int(math.log2(MLA_ROPE))) == (h & (rope_groups - 1)), qr, jnp.zeros_like(qr))
                refs_m = (kc_ref, vm_ref) + ((pkc_ref, pvm_ref) if has_prefix else (None, None))
            chains.append((jnp.concatenate(own + [extra], axis=0),) + refs_m + (vrows, mixer))

    qts = [c[0] for c in chains]

    def scores(key_refs, rows):
        return _scores([kr[0, rows, :] for kr in key_refs], qts)

    def update(sts, sc, val_refs, rows, **kw):
        return _online_update(sts, sc, [vr[0, c[5], rows] for vr, c in zip(val_refs, chains)], **kw)

    init = (jnp.full((1, tq), NEG, F32), jnp.zeros((1, tq), F32), jnp.zeros((FOX_HEAD_DIM, tq), F32))
    states = tuple(init for _ in chains)
    if has_prefix:
        pkeys, pvals = [c[3] for c in chains], [c[4] for c in chains]
        for pt in range(pkfa_ref.shape[1] // tkp):
            rows = slice(pt * tkp, (pt + 1) * tkp)
            states = update(states, scores(pkeys, rows), pvals, rows, bias=pref_bias)

    assert tq == tk
    own_keys, own_vals = [c[1] for c in chains], [c[2] for c in chains]
    tile_rows = (lambda t: pl.ds(pl.multiple_of(t * tk, tk), tk)) if nq > 1 else (lambda t: slice(0, tk))

    def put_scores(ref, t):
        for c, s in enumerate(scores(own_keys, tile_rows(t))):
            ref[c] = s

    def step(sts, ref, t, **kw):
        return update(sts, [ref[c] for c in range(len(chains))], own_vals, tile_rows(t), **kw)

    def finish(sts, ref):
        krow = lax.broadcasted_iota(jnp.int32, (tq, tq), 0)
        qcol = lax.broadcasted_iota(jnp.int32, (tq, tq), 1)
        chunk_shift = int(math.log2(CHUNK))
        masks = (krow <= qcol, (krow >> chunk_shift) <= (qcol >> chunk_shift))
        sts = step(sts, ref, i, masks=[masks[c[6]] for c in chains])
        outs = [acc / l for _, l, acc in sts]
        for o_ref, pair in ((of_ref, outs[:HEADS_PER_PAIR]), (om_ref, outs[HEADS_PER_PAIR:])):
            ot = jnp.concatenate(pair, axis=0)
            o_ref[0] = (ot.T if transpose_out else ot).astype(BF16)

    put_scores(sa_ref, 0)
    if nq == 1:
        finish(states, sa_ref)
    else:
        def body(u, sts):
            put_scores(sb_ref, 2 * u + 1)
            sts = step(sts, sa_ref, 2 * u)
            put_scores(sa_ref, 2 * u + 2)
            return step(sts, sb_ref, 2 * u + 1)

        states = lax.fori_loop(0, i >> 1, body, states)

        @pl.when((i & 1) == 0)
        def _():
            finish(states, sa_ref)

        @pl.when((i & 1) == 1)
        def _():
            put_scores(sb_ref, i)
            finish(step(states, sa_ref, i - 1), sb_ref)


def _attention(ops, prefix, *, tq, tk, tkp, n_pref=None):
    B, n, _ = ops['kfa'].shape
    assert n % tq == 0 and tq % tk == 0
    nq = n // tq
    transpose_out = tq % LANES == 0
    rope_cols_per_pair = LANES // (HEADS_PER_PAIR * MLA_ROPE)
    qspec = pl.BlockSpec((1, LANES, tq), lambda b, j, i: (b, j, i))
    qrspec = pl.BlockSpec((1, LANES, tq), lambda b, j, i: (b, j // rope_cols_per_pair, i))
    kspec = pl.BlockSpec((1, n, PAIR_W), lambda b, j, i: (b, 0, j))
    vspec = pl.BlockSpec((1, LANES, n), lambda b, j, i: (b, j, 0))
    args = [ops['qf'], ops['qn'], ops['qr'], ops['kfa'], ops['vf'], ops['kc'], ops['vm']]
    in_specs = [qspec, qspec, qrspec, kspec, vspec, kspec, vspec]
    block_bytes = 2 * (3 * tq * LANES + 2 * n * (PAIR_W + LANES) + 2 * tq * LANES)
    if prefix is not None:
        Bp, P, _ = prefix['kfa'].shape
        assert P % tkp == 0
        if Bp == 1:
            pb = lambda b: 0
        elif n_pref is not None:
            pb = lambda b: jnp.minimum(b, n_pref - 1)
        else:
            pb = lambda b: b
        pk = pl.BlockSpec((1, P, PAIR_W), lambda b, j, i: (pb(b), 0, j))
        pv = pl.BlockSpec((1, LANES, P), lambda b, j, i: (pb(b), j, 0))
        args += [prefix['kfa'], prefix['vf'], prefix['kc'], prefix['vm']]
        in_specs += [pk, pv, pk, pv]
        block_bytes += 2 * 2 * P * (PAIR_W + LANES)
    if transpose_out:
        ospec = pl.BlockSpec((1, tq, LANES), lambda b, j, i: (b, i, j))
        oshape = (B, n, N_PAIRS * LANES)
    else:
        ospec = pl.BlockSpec((1, LANES, tq), lambda b, j, i: (b, j, i))
        oshape = (B, N_PAIRS * LANES, n)
    temp_bytes = 4 * 6 * tq * max(tk, tkp if prefix is not None else tk) * 4 + 16 * tq * LANES * 4
    of, om = pl.pallas_call(
        functools.partial(_attn_kernel, tq=tq, tk=tk, tkp=tkp, has_prefix=prefix is not None, n_pref=n_pref,
                          nq=nq, transpose_out=transpose_out),
        grid=(B, N_PAIRS, nq), in_specs=in_specs, out_specs=[ospec, ospec],
        out_shape=[jax.ShapeDtypeStruct(oshape, BF16), jax.ShapeDtypeStruct(oshape, BF16)],
        scratch_shapes=[pltpu.VMEM((N_CHAINS, tk, tq), F32), pltpu.VMEM((N_CHAINS, tk, tq), F32)],
        compiler_params=pltpu.CompilerParams(
            dimension_semantics=("arbitrary", "arbitrary", "arbitrary"),
            vmem_limit_bytes=_vmem_limit(block_bytes, 2 * N_CHAINS * tk * tq * 4, temp_bytes)),
        name="attn",
    )(*args)
    if not transpose_out:
        of, om = jnp.swapaxes(of, 1, 2), jnp.swapaxes(om, 1, 2)
    return of, om


def _ffn_kernel(*refs, d_ff, fc, flat_group):
    if flat_group:
        (h_ref, of_ref, om_ref, g_ref, left_ref, wof_ref, wom_ref, wout_ref, gffn_ref, wup_ref,
         cw_ref, cb_ref, wdown_ref, gfin_ref, y_ref, u_ref, l1_ref, l2_ref) = refs
        l1_ref[...] = jnp.zeros_like(l1_ref)
        l2_ref[...] = jnp.zeros_like(l2_ref)
        for g in range(left_ref.shape[0]):
            r0 = g * flat_group
            l1_ref[r0:r0 + 1, :] = left_ref[g, 1:2, :]
            l2_ref[r0:r0 + 1, :] = left_ref[g, 0:1, :]
            l2_ref[r0 + 1:r0 + 2, :] = left_ref[g, 1:2, :]
    else:
        (h_ref, of_ref, om_ref, g_ref, left_ref, wof_ref, wom_ref, wout_ref, gffn_ref, wup_ref,
         cw_ref, cb_ref, wdown_ref, gfin_ref, y_ref, u_ref, carry_ref) = refs

        @pl.when(pl.program_id(1) == 0)
        def _():
            carry_ref[0:CONV_W - 1, :] = left_ref[0]

    ts, d = h_ref.shape[1], h_ref.shape[2]
    ya = jnp.dot(of_ref[0], wof_ref[...], preferred_element_type=F32)
    yb = jnp.dot(om_ref[0], wom_ref[...], preferred_element_type=F32)
    mix = g_ref[0, :, :d] * ya + g_ref[0, :, d:] * yb
    h1 = h_ref[0] + jnp.dot(mix.astype(BF16), wout_ref[...], preferred_element_type=F32)
    xn = _rms(h1, gffn_ref[...]).astype(BF16)

    row = lax.broadcasted_iota(jnp.int32, (ts, 1), 0)
    if flat_group:
        keep1 = ((row & (flat_group - 1)) >= 1).astype(F32)
        keep2 = ((row & (flat_group - 1)) >= 2).astype(F32)

    def conv(c0):
        cols = slice(c0, c0 + fc)
        u = jnp.dot(xn, wup_ref[:, cols], preferred_element_type=F32)
        u1 = pltpu.roll(u, 1, 0)
        u2 = pltpu.roll(u, 2, 0)
        if flat_group:
            u_ref[0, :, cols] = u
            u1 = u1 * keep1 + l1_ref[:, cols]
            u2 = u2 * keep2 + l2_ref[:, cols]
        else:
            prev2 = carry_ref[0:1, cols]
            prev1 = carry_ref[1:2, cols]
            u1 = jnp.where(row == 0, prev1, u1)
            u2 = jnp.where(row == 0, prev2, jnp.where(row == 1, prev1, u2))
            carry_ref[0:CONV_W - 1, cols] = u[ts - (CONV_W - 1):, :]
            u_ref[0, :, cols] = u[ts - (CONV_W - 1):, :]
        return cb_ref[:, cols] + u2 * cw_ref[0:1, cols] + u1 * cw_ref[1:2, cols] + u * cw_ref[2:3, cols]

    acc = jnp.zeros((ts, d), F32)
    for c in range(d_ff // fc):
        gate = conv(c * fc)
        val = conv(d_ff + c * fc)
        act = (jax.nn.silu(gate) * val).astype(BF16)
        acc = acc + jnp.dot(act, wdown_ref[c * fc:(c + 1) * fc, :], preferred_element_type=F32)
    y_ref[0] = _rms(h1 + acc, gfin_ref[...])


def _merge_ffn(h, of, om, gates, left, wts, *, ts, flat_group=0):
    B, n, d = h.shape
    d_ff = wts['w_down'].shape[0]
    fc = MXU_DIM
    assert n % ts == 0 and d_ff % fc == 0
    row = lambda w: pl.BlockSpec((1, ts, w), lambda b, t: (b, t, 0))
    consts = [wts[k] for k in ('w_o_fox', 'w_o_mla', 'w_out', 'g_ffn', 'w_up', 'conv_w', 'conv_b', 'w_down', 'g_fin')]
    const_specs = [_const_spec(c.shape) for c in consts]
    in_specs = [row(d), row(of.shape[2]), row(om.shape[2]), row(2 * d)]
    args = [h, of, om, gates, left]
    if flat_group:
        assert B == 1 and ts % flat_group == 0 and left.shape[0] * flat_group == n
        gpt = ts // flat_group
        in_specs.append(pl.BlockSpec((gpt, CONV_W - 1, 2 * d_ff), lambda b, t: (t, 0, 0)))
        u_shape, u_spec = (1, n, 2 * d_ff), row(2 * d_ff)
        scratch = [pltpu.VMEM((ts, 2 * d_ff), F32), pltpu.VMEM((ts, 2 * d_ff), F32)]
        const_extra = 2 * ts * 2 * d_ff * 4 + 2 * gpt * 8 * 2 * d_ff * 4
    else:
        bl = left.shape[0]
        in_specs.append(pl.BlockSpec((1, CONV_W - 1, 2 * d_ff), lambda b, t: (b if bl > 1 else 0, 0, 0)))
        u_shape = (B, CONV_W - 1, 2 * d_ff)
        u_spec = pl.BlockSpec((1, CONV_W - 1, 2 * d_ff), lambda b, t: (b, 0, 0))
        scratch = [pltpu.VMEM((8, 2 * d_ff), F32)]
        const_extra = 8 * 2 * d_ff * 4
    u_rows = ts if flat_group else 8
    block_bytes = ts * (d * 4 + (of.shape[2] + om.shape[2]) * 2 + 2 * d * 4 + d * 4) + u_rows * 2 * d_ff * 4
    const_bytes = sum(c.size * c.dtype.itemsize for c in consts) + const_extra
    temp_bytes = ts * d * 4 * 8 + ts * fc * 4 * 16
    y, u = pl.pallas_call(
        functools.partial(_ffn_kernel, d_ff=d_ff, fc=fc, flat_group=flat_group),
        grid=(B, n // ts), in_specs=in_specs + const_specs,
        out_specs=[row(d), u_spec],
        out_shape=[jax.ShapeDtypeStruct((B, n, d), F32), jax.ShapeDtypeStruct(u_shape, F32)],
        scratch_shapes=scratch,
        compiler_params=pltpu.CompilerParams(
            dimension_semantics=("arbitrary", "arbitrary"),
            vmem_limit_bytes=_vmem_limit(block_bytes, const_bytes, temp_bytes)),
        name="ffn",
    )(*args, *consts)
    return y, u


def _rope_tables(pos):
    half = MLA_ROPE // 2
    inv = ROPE_BASE ** (-jnp.arange(0, MLA_ROPE, 2, dtype=F32) / MLA_ROPE)
    ang = pos.astype(F32)[:, None] * inv[None, :]
    cos, sin = jnp.cos(ang), jnp.sin(ang)
    zero = jnp.zeros_like(sin)
    reps = LANES // MLA_ROPE
    ct = jnp.tile(jnp.concatenate([cos, cos], axis=1), (1, reps))
    sa = jnp.tile(jnp.concatenate([-sin, zero], axis=1), (1, reps))
    sb = jnp.tile(jnp.concatenate([zero, sin], axis=1), (1, reps))
    assert half * 2 == MLA_ROPE
    return ct, sa, sb


def _lower_tri(n, group):
    i = np.arange(n)
    m = (i[None, :] <= i[:, None]) & ((i[None, :] // group) == (i[:, None] // group))
    return jnp.asarray(m, BF16)


def _prepare_weights(norm_mix_g, w_in, b_forget, mla_q_norm_g, w_uq, mla_kv_norm_g, w_ukv, w_o_fox, w_o_mla, w_out,
                     norm_ffn_g, w_up, conv_w, conv_b, w_down, norm_final_g):
    d = w_in.shape[0]
    q_lora, kv_lora = mla_q_norm_g.shape[0], mla_kv_norm_g.shape[0]
    off_ff = 3 * FOX_W
    off_cq = off_ff + FOX_HEADS
    off_ckv = off_cq + q_lora
    off_kr = off_ckv + kv_lora
    off_gate = off_kr + MLA_ROPE
    pad = LANES - F_PARTS * FOX_HEADS
    w_ff = jnp.pad(jnp.tile(w_in[:, off_ff:off_cq], (1, F_PARTS)), ((0, 0), (0, pad)))
    w_a = jnp.concatenate([
        w_in[:, :FOX_W] * FOX_SCALE,
        w_in[:, FOX_W:off_ff], w_ff, w_in[:, off_cq:off_kr],
        jnp.tile(w_in[:, off_kr:off_gate], (1, LANES // MLA_ROPE)),
        w_in[:, off_gate:]], axis=1).astype(BF16)
    uq = w_uq.reshape(q_lora, MLA_HEADS, MLA_QK)
    w_uq_p = jnp.concatenate([uq[:, :, :MLA_NOPE].reshape(q_lora, -1), uq[:, :, MLA_NOPE:].reshape(q_lora, -1)], axis=1)
    ukv = w_ukv.reshape(kv_lora, MLA_HEADS, MLA_NOPE + MLA_V)
    w_ukv_p = jnp.concatenate([ukv[:, :, :MLA_NOPE].reshape(kv_lora, -1), ukv[:, :, MLA_NOPE:].reshape(kv_lora, -1)], axis=1)
    return dict(
        g_mix=norm_mix_g.reshape(1, d), w_a=w_a,
        b_exp=jnp.pad(jnp.tile(b_forget, F_PARTS), (0, pad)).reshape(1, LANES),
        g_q=mla_q_norm_g.reshape(1, q_lora), w_uq=w_uq_p.astype(BF16),
        g_kv=mla_kv_norm_g.reshape(1, kv_lora), w_ukv=w_ukv_p.astype(BF16),
        w_o_fox=w_o_fox.astype(BF16), w_o_mla=w_o_mla.astype(BF16), w_out=w_out.astype(BF16),
        g_ffn=norm_ffn_g.reshape(1, d), w_up=w_up.astype(BF16), conv_w=conv_w, conv_b=conv_b.reshape(1, -1),
        w_down=w_down.astype(BF16), g_fin=norm_final_g.reshape(1, d))


PROMPT_ROWS = 256
PROMPT_TQ = 256
SMALL_ROWS = LANES


def kernel(x_prompt, x_sample, cache_fox_k, cache_fox_v, cache_fox_logf, cache_mla_ckv, cache_mla_krope, state_ffn_conv, meta_tokens, norm_mix_g, w_in, b_forget, mla_q_norm_g, w_uq, mla_kv_norm_g, w_ukv, w_o_fox, w_o_mla, w_out, norm_ffn_g, w_up, conv_w, conv_b, w_down, norm_final_g):
    assert w_in.shape[0] == 1, "single-layer model"
    B, S, d = x_prompt.shape
    Bs, ns, _ = x_sample.shape
    P = cache_fox_k.shape[2]
    n_meta = meta_tokens.shape[0]
    assert n_meta == N_META == ns
    wts = _prepare_weights(norm_mix_g[0], w_in[0], b_forget[0], mla_q_norm_g[0], w_uq[0], mla_kv_norm_g[0], w_ukv[0],
                           w_o_fox[0], w_o_mla[0], w_out[0], norm_ffn_g[0], w_up[0], conv_w[0], conv_b[0], w_down[0],
                           norm_final_g)
    d_ff2 = w_up.shape[2]

    r_pad = -(-(Bs + 1) * ns // SMALL_ROWS) * SMALL_ROWS
    n_groups = r_pad // ns
    n_fill = r_pad - (Bs + 1) * ns
    rows = jnp.concatenate([x_sample.reshape(Bs * ns, d), jnp.zeros((n_fill, d), x_sample.dtype),
                            meta_tokens.astype(x_sample.dtype)], axis=0)[None]
    pos_small = jnp.concatenate([jnp.tile(P + jnp.arange(ns), Bs), jnp.zeros((n_fill,), jnp.int32), jnp.arange(n_meta)])
    zero_f = jnp.zeros((1, 8, LANES), F32)
    sm = _project(rows, _rope_tables(pos_small), zero_f, wts, _lower_tri(r_pad, ns), ts=r_pad, cs=r_pad)
    sm_b = {k: (jnp.swapaxes(v.reshape(v.shape[1], n_groups, ns), 0, 1) if k in TRANSPOSED
                else v.reshape(n_groups, ns, v.shape[-1])) for k, v in sm.items() if k != 'ftot'}
    live = lambda a: jnp.concatenate([a[:Bs], a[n_groups - 1:]], axis=0)

    lf_exp = jnp.pad(jnp.tile(cache_fox_logf[0], (1, 1, F_PARTS)), ((0, 0), (0, 0), (0, LANES - F_PARTS * FOX_HEADS)))
    prefix_s = _prefix_operands(
        cache_fox_k[0].reshape(Bs, P, FOX_W), cache_fox_v[0].reshape(Bs, P, FOX_W), lf_exp, cache_mla_ckv[0],
        jnp.tile(cache_mla_krope[0], (1, 1, LANES // MLA_ROPE)), wts['w_ukv'], _lower_tri(MXU_DIM, MXU_DIM), cs=MXU_DIM)
    of_s, om_s = _attention({k: live(v) for k, v in sm_b.items()}, prefix_s, tq=ns, tk=ns, tkp=P, n_pref=Bs)

    def spread(a):
        return jnp.concatenate([a[:Bs].reshape(Bs * ns, -1), jnp.zeros((n_fill, a.shape[-1]), a.dtype), a[Bs]], axis=0)[None]

    left_groups = jnp.concatenate([state_ffn_conv[0], jnp.zeros((n_groups - Bs, CONV_W - 1, d_ff2), F32)], axis=0)
    y_s, u_s = _merge_ffn(rows, spread(of_s), spread(om_s), sm['gate'], left_groups, wts, ts=SMALL_ROWS, flat_group=ns)
    y_sample = y_s[0, :Bs * ns].reshape(Bs, ns, d)
    u_groups = u_s.reshape(n_groups, ns, d_ff2)
    conv_s = u_groups[:Bs, ns - (CONV_W - 1):]
    left_meta = u_groups[n_groups - 1:, ns - (CONV_W - 1):]

    meta_ops = {k: sm_b[k][n_groups - 1:] for k in ('kfa', 'vf', 'kc', 'vm')}
    pos_p = n_meta + jnp.arange(S)
    pr = _project(x_prompt, _rope_tables(pos_p), sm['ftot'], wts, _lower_tri(MXU_DIM, MXU_DIM), ts=PROMPT_ROWS, cs=MXU_DIM)
    of_p, om_p = _attention(pr, meta_ops, tq=PROMPT_TQ, tk=PROMPT_TQ, tkp=n_meta)
    y_prompt, conv_p = _merge_ffn(x_prompt, of_p, om_p, pr['gate'], left_meta, wts, ts=PROMPT_ROWS)

    def with_meta(name, tail):
        meta = jnp.broadcast_to(sm_b[name][n_groups - 1:], (B, n_meta, sm_b[name].shape[-1]))
        return jnp.concatenate([meta, pr[name]], axis=1).reshape((1, B, n_meta + S) + tail)

    def sample_rows(name, tail):
        return sm_b[name][:Bs].reshape((1, Bs, ns) + tail)

    hd = (FOX_HEADS, FOX_HEAD_DIM)
    return (y_prompt, y_sample,
            with_meta('fk', hd), with_meta('fv', hd), with_meta('logf', (FOX_HEADS,)),
            with_meta('ckv', (cache_mla_ckv.shape[-1],)), with_meta('kr', (MLA_ROPE,)), conv_p[None],
            sample_rows('fk', hd), sample_rows('fv', hd), sample_rows('logf', (FOX_HEADS,)),
            sample_rows('ckv', (cache_mla_ckv.shape[-1],)), sample_rows('kr', (MLA_ROPE,)), conv_s[None])
```

```python
import functools
import math

import jax
import jax.numpy as jnp
import numpy as np
from jax import lax
from jax.experimental import pallas as pl
from jax.experimental.pallas import tpu as pltpu

F32 = jnp.float32
BF16 = jnp.bfloat16

N_META = 16
CHUNK = 64
EPS = 1e-6
NEG = -1e30
FOX_HEADS = 8
FOX_HEAD_DIM = 64
MLA_HEADS = 8
MLA_NOPE = 64
MLA_ROPE = 32
MLA_V = 64
MLA_QK = MLA_NOPE + MLA_ROPE
ROPE_BASE = 10000.0
CONV_W = 3

FOX_W = FOX_HEADS * FOX_HEAD_DIM
LOG2E = math.log2(math.e)
FOX_QSCALE = LOG2E / math.sqrt(FOX_HEAD_DIM)
MLA_QSCALE = LOG2E / math.sqrt(MLA_QK)

LANES = 128
MXU_DIM = 256
VMEM_BYTES_V7X = 64 * 1024 * 1024
F_PARTS = 3
HEADS_PER_PAIR = LANES // FOX_HEAD_DIM
N_PAIRS = FOX_HEADS // HEADS_PER_PAIR
PAIR_W = 2 * LANES
HALO = 8
GATE_CHUNKS = 4

C_FQ = 0
C_FK = C_FQ + FOX_W
C_FV = C_FK + FOX_W
C_FF = C_FV + FOX_W
C_CQ = C_FF + LANES


def _rms(x, g):
    return x * lax.rsqrt(jnp.mean(x * x, axis=-1, keepdims=True) + EPS) * g


def _split3(x):
    hi = x.astype(BF16)
    r = x - hi.astype(F32)
    mid = r.astype(BF16)
    lo = (r - mid.astype(F32)).astype(BF16)
    return hi, mid, lo


def _cumsum_rows(lf, tri, carry, cs):
    hi, mid, lo = _split3(lf)
    pieces = []
    for c in range(lf.shape[0] // cs):
        sl = slice(c * cs, (c + 1) * cs)
        fc = (jnp.dot(tri, hi[sl], preferred_element_type=F32)
              + jnp.dot(tri, mid[sl], preferred_element_type=F32)
              + jnp.dot(tri, lo[sl], preferred_element_type=F32)) + carry
        carry = fc[cs - 1:cs, :]
        pieces.append(fc)
    f = pieces[0] if len(pieces) == 1 else jnp.concatenate(pieces, axis=0)
    return f, carry


def _forget_columns(f):
    hi, mid, lo = _split3(f)
    lane = lax.broadcasted_iota(jnp.int32, f.shape, 1)
    zero = jnp.zeros_like(hi)
    return jnp.where(lane < FOX_HEADS, hi,
                     jnp.where(lane < 2 * FOX_HEADS, mid,
                               jnp.where(lane < F_PARTS * FOX_HEADS, lo, zero)))


def _store_pairs(ref, main, extra):
    for j in range(N_PAIRS):
        ref[0, :, PAIR_W * j:PAIR_W * j + LANES] = main[:, LANES * j:LANES * (j + 1)]
        ref[0, :, PAIR_W * j + LANES:PAIR_W * (j + 1)] = extra


def _store_cache(ref, val):
    lead = (0,) * (len(ref.shape) - 2)
    rows, width = val.shape
    if ref.shape[-1] == width:
        ref[lead] = val
    else:
        dh = ref.shape[-1]
        heads = width // dh
        assert ref.shape[-2] == rows * heads
        for h in range(heads):
            ref[lead + (pl.ds(h, rows, stride=heads), slice(None))] = val[:, h * dh:(h + 1) * dh]


def _rope(x, ct, sa, sb):
    return x * ct + pltpu.roll(x, LANES - MLA_ROPE // 2, 1) * sa + pltpu.roll(x, MLA_ROPE // 2, 1) * sb


def _proj_kernel(x_ref, ct_ref, sa_ref, sb_ref, finit_ref, gmix_ref, w_ref, bexp_ref, gq_ref, wuq_ref,
                 gkv_ref, wukv_ref, tri_ref,
                 fk_ref, fv_ref, logf_ref, ckv_ref, kr_ref, qf_ref, kfa_ref, vf_ref, qn_ref, qr_ref,
                 kc_ref, vm_ref, gate_ref, ftot_ref, carry_ref, *, cs, q_lora, kv_lora, d_model):
    t = pl.program_id(1)

    @pl.when(t == 0)
    def _():
        carry_ref[...] = finit_ref[0]

    c_ckv = C_CQ + q_lora
    c_kr = c_ckv + kv_lora
    c_gate = c_kr + LANES
    xb = _rms(x_ref[0], gmix_ref[...]).astype(BF16)

    def proj(a, b):
        return jnp.dot(xb, w_ref[:, a:b], preferred_element_type=F32)

    ct, sa, sb = ct_ref[...], sa_ref[...], sb_ref[...]
    nope_w = MLA_HEADS * MLA_NOPE
    cq = proj(C_CQ, c_ckv)
    ckv = proj(c_ckv, c_kr)
    ff = proj(C_FF, C_CQ)
    kr = proj(c_kr, c_gate)

    qf_ref[0] = (proj(C_FQ, C_FK) * FOX_QSCALE).T.astype(BF16)
    fk = proj(C_FK, C_FV)
    _store_cache(fk_ref, fk)
    fv = proj(C_FV, C_FF)
    _store_cache(fv_ref, fv)
    vf_ref[0] = fv.T.astype(BF16)

    cqn = _rms(cq, gq_ref[...]).astype(BF16)
    ckvn = _rms(ckv, gkv_ref[...])
    _store_cache(ckv_ref, ckvn)
    q = jnp.dot(cqn, wuq_ref[...], preferred_element_type=F32) * MLA_QSCALE
    kv = jnp.dot(ckvn.astype(BF16), wukv_ref[...], preferred_element_type=F32)

    lf = jax.nn.log_sigmoid(ff + bexp_ref[...])
    _store_cache(logf_ref, lf[:, :FOX_HEADS])
    f, carry = _cumsum_rows(lf, tri_ref[...], carry_ref[0:1, :], cs)
    carry_ref[0:1, :] = carry
    ftot_ref[0] = jnp.broadcast_to(carry, ftot_ref.shape[1:])

    gw = 2 * d_model // GATE_CHUNKS
    for c in range(GATE_CHUNKS):
        gate_ref[0, :, c * gw:(c + 1) * gw] = jax.nn.sigmoid(proj(c_gate + c * gw, c_gate + (c + 1) * gw))

    _store_pairs(kfa_ref, fk.astype(BF16), _forget_columns(f * LOG2E))
    qn_ref[0] = q[:, :nope_w].T.astype(BF16)
    for c in range(MLA_HEADS * MLA_ROPE // LANES):
        qr_ref[0, LANES * c:LANES * (c + 1), :] = _rope(
            q[:, nope_w + LANES * c:nope_w + LANES * (c + 1)], ct, sa, sb).T.astype(BF16)
    vm_ref[0] = kv[:, nope_w:].T.astype(BF16)
    kr4 = _rope(kr, ct, sa, sb)
    _store_cache(kr_ref, kr4[:, :MLA_ROPE])
    _store_pairs(kc_ref, kv[:, :nope_w].astype(BF16), kr4.astype(BF16))


TRANSPOSED = ('qf', 'vf', 'qn', 'qr', 'vm')
CACHE_KEYS = ('fk', 'fv', 'logf', 'ckv', 'kr')


def _const_spec(shape):
    return pl.BlockSpec(shape, lambda *_: (0,) * len(shape), pipeline_mode=pl.Buffered(1))


def _vmem_limit(block_bytes, const_bytes, temp_bytes):
    need = 2 * block_bytes + const_bytes + temp_bytes
    assert need < VMEM_BYTES_V7X, need
    return int(need)


def _project(x, tables, finit, wts, tri, *, ts, cs, cache_row0=None):
    B, n, d = x.shape
    assert n % ts == 0 and ts % cs == 0
    nt = n // ts
    q_lora = wts['g_q'].shape[1]
    kv_lora = wts['g_kv'].shape[1]
    ct, sa, sb = tables
    row = lambda w: pl.BlockSpec((1, ts, w), lambda b, t: (b, t, 0))
    tab = pl.BlockSpec((ts, LANES), lambda b, t: (t, 0))
    in_specs = [row(d), tab, tab, tab,
                pl.BlockSpec((1, 8, LANES), lambda b, t: (0, 0, 0)),
                _const_spec(wts['g_mix'].shape), _const_spec(wts['w_a'].shape), _const_spec(wts['b_exp'].shape),
                _const_spec(wts['g_q'].shape), _const_spec(wts['w_uq'].shape), _const_spec(wts['g_kv'].shape),
                _const_spec(wts['w_ukv'].shape), _const_spec(tri.shape)]
    widths = dict(fk=(FOX_W, F32), fv=(FOX_W, F32), logf=(FOX_HEADS, F32), ckv=(kv_lora, F32), kr=(MLA_ROPE, F32),
                  qf=(FOX_W, BF16), kfa=(N_PAIRS * PAIR_W, BF16), vf=(FOX_W, BF16),
                  qn=(MLA_HEADS * MLA_NOPE, BF16), qr=(MLA_HEADS * MLA_ROPE, BF16),
                  kc=(N_PAIRS * PAIR_W, BF16), vm=(MLA_HEADS * MLA_V, BF16), gate=(2 * d, F32))
    col = lambda w: pl.BlockSpec((1, w, ts), lambda b, t: (b, 0, t))
    out_shape = [jax.ShapeDtypeStruct((B, w, n) if k in TRANSPOSED else (B, n, w), dt) for k, (w, dt) in widths.items()]
    out_specs = [col(w) if k in TRANSPOSED else row(w) for k, (w, _) in widths.items()]
    if cache_row0 is not None:
        assert cache_row0 % 8 == 0 and ts % 8 == 0
        for idx, k in enumerate(CACHE_KEYS):
            heads, w = (FOX_HEADS, FOX_HEAD_DIM) if k in ('fk', 'fv') else (1, widths[k][0])
            out_shape[idx] = jax.ShapeDtypeStruct((1, B, (cache_row0 + n) * heads, w), F32)
            out_specs[idx] = pl.BlockSpec(
                tuple(pl.Element(s) for s in (1, 1, ts * heads, w)),
                lambda b, t, heads=heads: (0, b, pl.multiple_of((cache_row0 + t * ts) * heads, 8), 0))
    out_shape.append(jax.ShapeDtypeStruct((B, 8, LANES), F32))
    out_specs.append(pl.BlockSpec((1, 8, LANES), lambda b, t: (b, 0, 0)))
    block_bytes = ts * (d * 4 + 3 * LANES * 4 + sum(w * jnp.dtype(dt).itemsize for w, dt in widths.values()))
    const_bytes = sum(int(np.prod(wts[k].shape)) * wts[k].dtype.itemsize for k in ('w_a', 'w_uq', 'w_ukv')) + tri.size * 2
    temp_bytes = ts * (2 * d + 2 * d) * 4 * 2
    outs = pl.pallas_call(
        functools.partial(_proj_kernel, cs=cs, q_lora=q_lora, kv_lora=kv_lora, d_model=d),
        grid=(B, nt), in_specs=in_specs, out_specs=out_specs, out_shape=out_shape,
        scratch_shapes=[pltpu.VMEM((8, LANES), F32)],
        compiler_params=pltpu.CompilerParams(
            dimension_semantics=("arbitrary", "arbitrary"),
            vmem_limit_bytes=_vmem_limit(block_bytes, const_bytes, temp_bytes)),
        name="proj",
    )(x, ct, sa, sb, finit, wts['g_mix'], wts['w_a'], wts['b_exp'], wts['g_q'], wts['w_uq'], wts['g_kv'],
      wts['w_ukv'], tri)
    res = dict(zip(widths.keys(), outs[:-1]))
    res['ftot'] = outs[-1]
    return res


def _fill_kernel(*refs):
    n = len(refs) // 3
    for src, dst in zip(refs[:n], refs[2 * n:]):
        _store_cache(dst, src[...])


def _fill_rows(caches, head_rows):
    n = len(caches)
    B = caches[0].shape[1]
    dst_specs = [pl.BlockSpec((1, 1, h.shape[0] * h.shape[1] // c.shape[3], c.shape[3]), lambda b: (0, b, 0, 0))
                 for c, h in zip(caches, head_rows)]
    outs = pl.pallas_call(
        _fill_kernel, grid=(B,),
        in_specs=[_const_spec(h.shape) for h in head_rows] + [pl.BlockSpec(memory_space=pl.ANY)] * n,
        out_specs=dst_specs,
        out_shape=[jax.ShapeDtypeStruct(c.shape, c.dtype) for c in caches],
        input_output_aliases={n + k: k for k in range(n)},
        compiler_params=pltpu.CompilerParams(dimension_semantics=("arbitrary",)),
        name="fill",
    )(*head_rows, *caches)
    return outs


def _prefix_kernel(ck_ref, cv_ref, lf_ref, ckv_ref, kr4_ref, wukv_ref, tri_ref,
                   kfa_ref, vf_ref, kc_ref, vm_ref, *, cs):
    f, total = _cumsum_rows(lf_ref[0], tri_ref[...], jnp.zeros((1, LANES), F32), cs)
    _store_pairs(kfa_ref, ck_ref[0].astype(BF16), _forget_columns((f - total) * LOG2E))
    vf_ref[0] = cv_ref[0].T.astype(BF16)
    kv = jnp.dot(ckv_ref[0].astype(BF16), wukv_ref[...], preferred_element_type=F32)
    nope_w = MLA_HEADS * MLA_NOPE
    vm_ref[0] = kv[:, nope_w:].T.astype(BF16)
    _store_pairs(kc_ref, kv[:, :nope_w].astype(BF16), kr4_ref[0].astype(BF16))


def _prefix_operands(ck, cv, lf_exp, cckv, kr4, w_ukv, tri, *, cs):
    B, P, _ = ck.shape
    row = lambda w: pl.BlockSpec((1, P, w), lambda b: (b, 0, 0))
    col = lambda w: pl.BlockSpec((1, w, P), lambda b: (b, 0, 0))
    widths = dict(kfa=N_PAIRS * PAIR_W, vf=FOX_W, kc=N_PAIRS * PAIR_W, vm=MLA_HEADS * MLA_V)
    in_w = (ck.shape[2], cv.shape[2], LANES, cckv.shape[2], LANES)
    block_bytes = P * (sum(in_w) * 4 + sum(widths.values()) * 2)
    outs = pl.pallas_call(
        functools.partial(_prefix_kernel, cs=cs),
        grid=(B,),
        in_specs=[row(w) for w in in_w] + [_const_spec(w_ukv.shape), _const_spec(tri.shape)],
        out_specs=[col(w) if k in TRANSPOSED else row(w) for k, w in widths.items()],
        out_shape=[jax.ShapeDtypeStruct((B, w, P) if k in TRANSPOSED else (B, P, w), BF16) for k, w in widths.items()],
        compiler_params=pltpu.CompilerParams(
            dimension_semantics=("arbitrary",),
            vmem_limit_bytes=_vmem_limit(block_bytes, w_ukv.size * 2 + tri.size * 2, P * 1024 * 4 * 3)),
        name="prefix",
    )(ck, cv, lf_exp, cckv, kr4, w_ukv, tri)
    return dict(zip(widths.keys(), outs))


def _scores(ks, qts):
    return tuple(jnp.dot(k, qt, preferred_element_type=F32) for k, qt in zip(ks, qts))


def _online_update(states, scores, vts, masks=None, bias=None):
    mids = []
    for c, ((m, l, acc), s) in enumerate(zip(states, scores)):
        if masks is not None:
            s = jnp.where(masks[c], s, NEG)
        if bias is not None:
            s = s + bias
        m_new = jnp.maximum(m, jnp.max(s, axis=0, keepdims=True))
        alpha = jnp.exp2(m - m_new)
        p = jnp.exp2(s - m_new)
        mids.append((m_new, alpha * l + jnp.sum(p, axis=0, keepdims=True), alpha * acc, p.astype(BF16)))
    return tuple((m_new, l, acc + jnp.dot(vt, p, preferred_element_type=F32))
                 for (m_new, l, acc, p), vt in zip(mids, vts))


def _attn_kernel(*refs, tq, tk, tkp, nq, pps, has_prefix, n_pref, transpose_out):
    if has_prefix:
        (qf_ref, qn_ref, qr_ref, kfa_ref, vf_ref, kc_ref, vm_ref,
         pkfa_ref, pvf_ref, pkc_ref, pvm_ref, of_ref, om_ref, sa_ref, sb_ref) = refs
    else:
        qf_ref, qn_ref, qr_ref, kfa_ref, vf_ref, kc_ref, vm_ref, of_ref, om_ref, sa_ref, sb_ref = refs
        pkfa_ref = pvf_ref = pkc_ref = pvm_ref = None
    b, j, i = pl.program_id(0), pl.program_id(1), pl.program_id(2)
    pref_bias = None if (not has_prefix or n_pref is None) else jnp.where(b < n_pref, 0.0, NEG)

    frow = lax.broadcasted_iota(jnp.int32, (LANES, tq), 0)
    zero_half = jnp.zeros((FOX_HEAD_DIM, tq), BF16)
    rope_groups = LANES // MLA_ROPE
    rope_rows_per_step = pps * HEADS_PER_PAIR * MLA_ROPE
    chains = []
    for mixer in range(2):
        for p in range(pps):
            for s in range(HEADS_PER_PAIR):
                h = HEADS_PER_PAIR * (pps * j + p) + s
                vrows = slice(p * LANES + s * FOX_HEAD_DIM, p * LANES + (s + 1) * FOX_HEAD_DIM)
                own = [zero_half] * HEADS_PER_PAIR
                own[s] = (qf_ref if mixer == 0 else qn_ref)[0, vrows, :]
                if mixer == 0:
                    extra = jnp.where((frow < F_PARTS * FOX_HEADS) & ((frow & (FOX_HEADS - 1)) == h),
                                      -1.0, 0.0).astype(BF16)
                else:
                    r0 = (p * HEADS_PER_PAIR * MLA_ROPE // LANES) * LANES if rope_rows_per_step > LANES else 0
                    qr = qr_ref[0, r0:r0 + LANES, :]
                    extra = jnp.where((frow >> int(math.log2(MLA_ROPE))) == (h & (rope_groups - 1)), qr, jnp.zeros_like(qr))
                chains.append((jnp.concatenate(own + [extra], axis=0), slice(p * PAIR_W, (p + 1) * PAIR_W), vrows, mixer))
    qts = [c[0] for c in chains]
    key_refs, val_refs = (kfa_ref, kc_ref), (vf_ref, vm_ref)
    pkey_refs, pval_refs = (pkfa_ref, pkc_ref), (pvf_ref, pvm_ref)

    def scores(krefs, rows):
        return _scores([krefs[c[3]][0, rows, c[1]] for c in chains], qts)

    def update(sts, sc, vrefs, rows, **kw):
        return _online_update(sts, sc, [vrefs[c[3]][0, c[2], rows] for c in chains], **kw)

    assert tq == tk
    tile_rows = (lambda t: pl.ds(pl.multiple_of(t * tk, tk), tk)) if nq > 1 else (lambda t: slice(0, tk))

    def put_scores(ref, t):
        for c, s in enumerate(scores(key_refs, tile_rows(t))):
            ref[c] = s

    def step(sts, ref, t, **kw):
        return update(sts, [ref[c] for c in range(len(chains))], val_refs, tile_rows(t), **kw)

    def finish(sts, ref):
        krow = lax.broadcasted_iota(jnp.int32, (tq, tq), 0)
        qcol = lax.broadcasted_iota(jnp.int32, (tq, tq), 1)
        chunk_shift = int(math.log2(CHUNK))
        masks = (krow <= qcol, (krow >> chunk_shift) <= (qcol >> chunk_shift))
        sts = step(sts, ref, i, masks=[masks[c[3]] for c in chains])
        outs = [acc / l for _, l, acc in sts]
        half = len(outs) // 2
        for o_ref, group in ((of_ref, outs[:half]), (om_ref, outs[half:])):
            ot = jnp.concatenate(group, axis=0)
            o_ref[0] = (ot.T if transpose_out else ot).astype(BF16)

    init = (jnp.full((1, tq), NEG, F32), jnp.zeros((1, tq), F32), jnp.zeros((FOX_HEAD_DIM, tq), F32))
    states = tuple(init for _ in chains)
    if has_prefix:
        n_ptiles = pkfa_ref.shape[1] // tkp
        psc = [scores(pkey_refs, slice(pt * tkp, (pt + 1) * tkp)) for pt in range(n_ptiles)]
    put_scores(sa_ref, 0)
    if has_prefix:
        for pt in range(n_ptiles):
            states = update(states, psc[pt], pval_refs, slice(pt * tkp, (pt + 1) * tkp), bias=pref_bias)
    if nq == 1:
        finish(states, sa_ref)
    else:
        def body(u, sts):
            put_scores(sb_ref, 2 * u + 1)
            sts = step(sts, sa_ref, 2 * u)
            put_scores(sa_ref, 2 * u + 2)
            return step(sts, sb_ref, 2 * u + 1)

        states = lax.fori_loop(0, i >> 1, body, states)

        @pl.when((i & 1) == 0)
        def _():
            finish(states, sa_ref)

        @pl.when((i & 1) == 1)
        def _():
            put_scores(sb_ref, i)
            finish(step(states, sa_ref, i - 1), sb_ref)


def _attention(ops, prefix, *, tq, tk, tkp, pps, n_pref=None):
    B, n, _ = ops['kfa'].shape
    assert n % tq == 0 and tq % tk == 0 and N_PAIRS % pps == 0
    nq = n // tq
    transpose_out = tq % LANES == 0
    rope_w = max(LANES, pps * HEADS_PER_PAIR * MLA_ROPE)
    steps_per_rope_block = rope_w // (pps * HEADS_PER_PAIR * MLA_ROPE)
    qspec = pl.BlockSpec((1, pps * LANES, tq), lambda b, j, i: (b, j, i))
    qrspec = pl.BlockSpec((1, rope_w, tq), lambda b, j, i: (b, j // steps_per_rope_block, i))
    kspec = pl.BlockSpec((1, n, pps * PAIR_W), lambda b, j, i: (b, 0, j))
    vspec = pl.BlockSpec((1, pps * LANES, n), lambda b, j, i: (b, j, 0))
    args = [ops['qf'], ops['qn'], ops['qr'], ops['kfa'], ops['vf'], ops['kc'], ops['vm']]
    in_specs = [qspec, qspec, qrspec, kspec, vspec, kspec, vspec]
    block_bytes = 2 * pps * (3 * tq * LANES + 2 * n * (PAIR_W + LANES) + 2 * tq * LANES)
    if prefix is not None:
        Bp, P, _ = prefix['kfa'].shape
        assert P % tkp == 0
        if Bp == 1:
            pb = lambda b: 0
        elif n_pref is not None:
            pb = lambda b: jnp.minimum(b, n_pref - 1)
        else:
            pb = lambda b: b
        pk = pl.BlockSpec((1, P, pps * PAIR_W), lambda b, j, i: (pb(b), 0, j))
        pv = pl.BlockSpec((1, pps * LANES, P), lambda b, j, i: (pb(b), j, 0))
        args += [prefix['kfa'], prefix['vf'], prefix['kc'], prefix['vm']]
        in_specs += [pk, pv, pk, pv]
        block_bytes += 2 * 2 * pps * P * (PAIR_W + LANES)
    if transpose_out:
        ospec = pl.BlockSpec((1, tq, pps * LANES), lambda b, j, i: (b, i, j))
        oshape = (B, n, N_PAIRS * LANES)
    else:
        ospec = pl.BlockSpec((1, pps * LANES, tq), lambda b, j, i: (b, j, i))
        oshape = (B, N_PAIRS * LANES, n)
    n_chains = 2 * HEADS_PER_PAIR * pps
    temp_bytes = n_chains * 6 * tq * max(tk, tkp if prefix is not None else tk) * 4 + 4 * n_chains * tq * LANES * 4
    of, om = pl.pallas_call(
        functools.partial(_attn_kernel, tq=tq, tk=tk, tkp=tkp, has_prefix=prefix is not None, n_pref=n_pref,
                          nq=nq, pps=pps, transpose_out=transpose_out),
        grid=(B, N_PAIRS // pps, nq), in_specs=in_specs, out_specs=[ospec, ospec],
        out_shape=[jax.ShapeDtypeStruct(oshape, BF16), jax.ShapeDtypeStruct(oshape, BF16)],
        scratch_shapes=[pltpu.VMEM((n_chains, tk, tq), F32), pltpu.VMEM((n_chains, tk, tq), F32)],
        compiler_params=pltpu.CompilerParams(
            dimension_semantics=("arbitrary", "arbitrary", "arbitrary"),
            vmem_limit_bytes=_vmem_limit(block_bytes, 2 * n_chains * tk * tq * 4, temp_bytes)),
        name="attn",
    )(*args)
    if not transpose_out:
        of, om = jnp.swapaxes(of, 1, 2), jnp.swapaxes(om, 1, 2)
    return of, om


def _ffn_kernel(*refs, d_ff, fc, flat_group):
    if flat_group:
        (h_ref, of_ref, om_ref, g_ref, left_ref, wof_ref, wom_ref, wout_ref, gffn_ref, wup_ref,
         cw_ref, cb_ref, wdown_ref, gfin_ref, y_ref, u_ref, l1_ref, l2_ref, _) = refs
        l1_ref[...] = jnp.zeros_like(l1_ref)
        l2_ref[...] = jnp.zeros_like(l2_ref)
        for g in range(left_ref.shape[0]):
            r0 = g * flat_group
            l1_ref[r0:r0 + 1, :] = left_ref[g, 1:2, :]
            l2_ref[r0:r0 + 1, :] = left_ref[g, 0:1, :]
            l2_ref[r0 + 1:r0 + 2, :] = left_ref[g, 1:2, :]
        ub_ref = refs[-1]
        ub_ref[0:HALO, :] = jnp.zeros((HALO, ub_ref.shape[1]), F32)
    else:
        (h_ref, of_ref, om_ref, g_ref, left_ref, wof_ref, wom_ref, wout_ref, gffn_ref, wup_ref,
         cw_ref, cb_ref, wdown_ref, gfin_ref, y_ref, u_ref, ub_ref) = refs

        @pl.when(pl.program_id(1) == 0)
        def _():
            ub_ref[HALO - (CONV_W - 1):HALO, :] = left_ref[0]

    ts, d = h_ref.shape[1], h_ref.shape[2]
    ya = jnp.dot(of_ref[0], wof_ref[...], preferred_element_type=F32)
    yb = jnp.dot(om_ref[0], wom_ref[...], preferred_element_type=F32)
    mix = g_ref[0, :, :d] * ya + g_ref[0, :, d:] * yb
    h1 = h_ref[0] + jnp.dot(mix.astype(BF16), wout_ref[...], preferred_element_type=F32)
    xn = _rms(h1, gffn_ref[...]).astype(BF16)

    if flat_group:
        row = lax.broadcasted_iota(jnp.int32, (ts, 1), 0)
        keep1 = ((row & (flat_group - 1)) >= 1).astype(F32)
        keep2 = ((row & (flat_group - 1)) >= 2).astype(F32)

    def up(c0):
        return jnp.dot(xn, wup_ref[:, c0:c0 + fc], preferred_element_type=F32)

    def conv(u, c0):
        cols = slice(c0, c0 + fc)
        ub_ref[HALO:HALO + ts, cols] = u
        u1 = ub_ref[HALO - 1:HALO - 1 + ts, cols]
        u2 = ub_ref[HALO - 2:HALO - 2 + ts, cols]
        if flat_group:
            u_ref[0, :, cols] = u
            u1 = u1 * keep1 + l1_ref[:, cols]
            u2 = u2 * keep2 + l2_ref[:, cols]
        return cb_ref[:, cols] + u2 * cw_ref[0:1, cols] + u1 * cw_ref[1:2, cols] + u * cw_ref[2:3, cols]

    n_chunks = d_ff // fc
    acc = jnp.zeros((ts, d), F32)
    u_next = (up(0), up(d_ff))
    for c in range(n_chunks):
        u_gate, u_val = u_next
        if c + 1 < n_chunks:
            u_next = (up((c + 1) * fc), up(d_ff + (c + 1) * fc))
        act = (jax.nn.silu(conv(u_gate, c * fc)) * conv(u_val, d_ff + c * fc)).astype(BF16)
        acc = acc + jnp.dot(act, wdown_ref[c * fc:(c + 1) * fc, :], preferred_element_type=F32)
    y_ref[0] = _rms(h1 + acc, gfin_ref[...])
    if not flat_group:
        tail = ub_ref[HALO + ts - (CONV_W - 1):HALO + ts, :]
        u_ref[0] = tail
        ub_ref[HALO - (CONV_W - 1):HALO, :] = tail


def _merge_ffn(h, of, om, gates, left, wts, *, ts, flat_group=0):
    B, n, d = h.shape
    d_ff = wts['w_down'].shape[0]
    fc = MXU_DIM
    assert n % ts == 0 and d_ff % fc == 0
    row = lambda w: pl.BlockSpec((1, ts, w), lambda b, t: (b, t, 0))
    consts = [wts[k] for k in ('w_o_fox', 'w_o_mla', 'w_out', 'g_ffn', 'w_up', 'conv_w', 'conv_b', 'w_down', 'g_fin')]
    const_specs = [_const_spec(c.shape) for c in consts]
    in_specs = [row(d), row(of.shape[2]), row(om.shape[2]), row(2 * d)]
    args = [h, of, om, gates, left]
    if flat_group:
        assert B == 1 and ts % flat_group == 0 and left.shape[0] * flat_group == n
        gpt = ts // flat_group
        in_specs.append(pl.BlockSpec((gpt, CONV_W - 1, 2 * d_ff), lambda b, t: (t, 0, 0)))
        u_shape, u_spec = (1, n, 2 * d_ff), row(2 * d_ff)
        scratch = [pltpu.VMEM((ts, 2 * d_ff), F32), pltpu.VMEM((ts, 2 * d_ff), F32)]
        const_extra = 2 * ts * 2 * d_ff * 4 + 2 * gpt * 8 * 2 * d_ff * 4
    else:
        bl = left.shape[0]
        in_specs.append(pl.BlockSpec((1, CONV_W - 1, 2 * d_ff), lambda b, t: (b if bl > 1 else 0, 0, 0)))
        u_shape = (B, CONV_W - 1, 2 * d_ff)
        u_spec = pl.BlockSpec((1, CONV_W - 1, 2 * d_ff), lambda b, t: (b, 0, 0))
        scratch = []
        const_extra = 0
    scratch.append(pltpu.VMEM((HALO + ts, 2 * d_ff), F32))
    const_extra += (HALO + ts) * 2 * d_ff * 4
    u_rows = ts if flat_group else 8
    block_bytes = ts * (d * 4 + (of.shape[2] + om.shape[2]) * 2 + 2 * d * 4 + d * 4) + u_rows * 2 * d_ff * 4
    const_bytes = sum(c.size * c.dtype.itemsize for c in consts) + const_extra
    temp_bytes = ts * d * 4 * 8 + ts * fc * 4 * 16
    y, u = pl.pallas_call(
        functools.partial(_ffn_kernel, d_ff=d_ff, fc=fc, flat_group=flat_group),
        grid=(B, n // ts), in_specs=in_specs + const_specs,
        out_specs=[row(d), u_spec],
        out_shape=[jax.ShapeDtypeStruct((B, n, d), F32), jax.ShapeDtypeStruct(u_shape, F32)],
        scratch_shapes=scratch,
        compiler_params=pltpu.CompilerParams(
            dimension_semantics=("arbitrary", "arbitrary"),
            vmem_limit_bytes=_vmem_limit(block_bytes, const_bytes, temp_bytes)),
        name="ffn",
    )(*args, *consts)
    return y, u


def _rope_tables(pos):
    half = MLA_ROPE // 2
    inv = ROPE_BASE ** (-jnp.arange(0, MLA_ROPE, 2, dtype=F32) / MLA_ROPE)
    ang = pos.astype(F32)[:, None] * inv[None, :]
    cos, sin = jnp.cos(ang), jnp.sin(ang)
    zero = jnp.zeros_like(sin)
    reps = LANES // MLA_ROPE
    ct = jnp.tile(jnp.concatenate([cos, cos], axis=1), (1, reps))
    sa = jnp.tile(jnp.concatenate([-sin, zero], axis=1), (1, reps))
    sb = jnp.tile(jnp.concatenate([zero, sin], axis=1), (1, reps))
    assert half * 2 == MLA_ROPE
    return ct, sa, sb


def _lower_tri(n, group):
    i = np.arange(n)
    m = (i[None, :] <= i[:, None]) & ((i[None, :] // group) == (i[:, None] // group))
    return jnp.asarray(m, BF16)


def _prepare_weights(norm_mix_g, w_in, b_forget, mla_q_norm_g, w_uq, mla_kv_norm_g, w_ukv, w_o_fox, w_o_mla, w_out,
                     norm_ffn_g, w_up, conv_w, conv_b, w_down, norm_final_g):
    d = w_in.shape[0]
    q_lora, kv_lora = mla_q_norm_g.shape[0], mla_kv_norm_g.shape[0]
    off_ff = 3 * FOX_W
    off_cq = off_ff + FOX_HEADS
    off_ckv = off_cq + q_lora
    off_kr = off_ckv + kv_lora
    off_gate = off_kr + MLA_ROPE
    pad = LANES - F_PARTS * FOX_HEADS
    w_ff = jnp.pad(jnp.tile(w_in[:, off_ff:off_cq], (1, F_PARTS)), ((0, 0), (0, pad)))
    w_a = jnp.concatenate([
        w_in[:, :off_ff], w_ff, w_in[:, off_cq:off_kr],
        jnp.tile(w_in[:, off_kr:off_gate], (1, LANES // MLA_ROPE)),
        w_in[:, off_gate:]], axis=1).astype(BF16)
    uq = w_uq.reshape(q_lora, MLA_HEADS, MLA_QK)
    w_uq_p = jnp.concatenate([uq[:, :, :MLA_NOPE].reshape(q_lora, -1), uq[:, :, MLA_NOPE:].reshape(q_lora, -1)], axis=1)
    ukv = w_ukv.reshape(kv_lora, MLA_HEADS, MLA_NOPE + MLA_V)
    w_ukv_p = jnp.concatenate([ukv[:, :, :MLA_NOPE].reshape(kv_lora, -1), ukv[:, :, MLA_NOPE:].reshape(kv_lora, -1)], axis=1)
    return dict(
        g_mix=norm_mix_g.reshape(1, d), w_a=w_a,
        b_exp=jnp.pad(jnp.tile(b_forget, F_PARTS), (0, pad)).reshape(1, LANES),
        g_q=mla_q_norm_g.reshape(1, q_lora), w_uq=w_uq_p.astype(BF16),
        g_kv=mla_kv_norm_g.reshape(1, kv_lora), w_ukv=w_ukv_p.astype(BF16),
        w_o_fox=w_o_fox.astype(BF16), w_o_mla=w_o_mla.astype(BF16), w_out=w_out.astype(BF16),
        g_ffn=norm_ffn_g.reshape(1, d), w_up=w_up.astype(BF16), conv_w=conv_w, conv_b=conv_b.reshape(1, -1),
        w_down=w_down.astype(BF16), g_fin=norm_final_g.reshape(1, d))


PROMPT_ROWS = 256
PROMPT_TQ = 256
PROMPT_PAIRS = 2
SMALL_ROWS = LANES


def kernel(x_prompt, x_sample, cache_fox_k, cache_fox_v, cache_fox_logf, cache_mla_ckv, cache_mla_krope, state_ffn_conv, meta_tokens, norm_mix_g, w_in, b_forget, mla_q_norm_g, w_uq, mla_kv_norm_g, w_ukv, w_o_fox, w_o_mla, w_out, norm_ffn_g, w_up, conv_w, conv_b, w_down, norm_final_g):
    assert w_in.shape[0] == 1, "single-layer model"
    B, S, d = x_prompt.shape
    Bs, ns, _ = x_sample.shape
    P = cache_fox_k.shape[2]
    n_meta = meta_tokens.shape[0]
    assert n_meta == N_META == ns
    wts = _prepare_weights(norm_mix_g[0], w_in[0], b_forget[0], mla_q_norm_g[0], w_uq[0], mla_kv_norm_g[0], w_ukv[0],
                           w_o_fox[0], w_o_mla[0], w_out[0], norm_ffn_g[0], w_up[0], conv_w[0], conv_b[0], w_down[0],
                           norm_final_g)
    d_ff2 = w_up.shape[2]

    r_pad = -(-(Bs + 1) * ns // SMALL_ROWS) * SMALL_ROWS
    n_groups = r_pad // ns
    n_fill = r_pad - (Bs + 1) * ns
    rows = jnp.concatenate([x_sample.reshape(Bs * ns, d), jnp.zeros((n_fill, d), x_sample.dtype),
                            meta_tokens.astype(x_sample.dtype)], axis=0)[None]
    pos_small = jnp.concatenate([jnp.tile(P + jnp.arange(ns), Bs), jnp.zeros((n_fill,), jnp.int32), jnp.arange(n_meta)])
    zero_f = jnp.zeros((1, 8, LANES), F32)
    sm = _project(rows, _rope_tables(pos_small), zero_f, wts, _lower_tri(r_pad, ns), ts=r_pad, cs=r_pad)
    sm_b = {k: (jnp.swapaxes(v.reshape(v.shape[1], n_groups, ns), 0, 1) if k in TRANSPOSED
                else v.reshape(n_groups, ns, v.shape[-1])) for k, v in sm.items() if k != 'ftot'}
    live = lambda a: jnp.concatenate([a[:Bs], a[n_groups - 1:]], axis=0)

    lf_exp = jnp.pad(jnp.tile(cache_fox_logf[0], (1, 1, F_PARTS)), ((0, 0), (0, 0), (0, LANES - F_PARTS * FOX_HEADS)))
    prefix_s = _prefix_operands(
        cache_fox_k[0].reshape(Bs, P, FOX_W), cache_fox_v[0].reshape(Bs, P, FOX_W), lf_exp, cache_mla_ckv[0],
        jnp.tile(cache_mla_krope[0], (1, 1, LANES // MLA_ROPE)), wts['w_ukv'], _lower_tri(MXU_DIM, MXU_DIM), cs=MXU_DIM)
    of_s, om_s = _attention({k: live(v) for k, v in sm_b.items()}, prefix_s, tq=ns, tk=ns, tkp=P, pps=1, n_pref=Bs)

    def spread(a):
        return jnp.concatenate([a[:Bs].reshape(Bs * ns, -1), jnp.zeros((n_fill, a.shape[-1]), a.dtype), a[Bs]], axis=0)[None]

    left_groups = jnp.concatenate([state_ffn_conv[0], jnp.zeros((n_groups - Bs, CONV_W - 1, d_ff2), F32)], axis=0)
    y_s, u_s = _merge_ffn(rows, spread(of_s), spread(om_s), sm['gate'], left_groups, wts, ts=SMALL_ROWS, flat_group=ns)
    y_sample = y_s[0, :Bs * ns].reshape(Bs, ns, d)
    u_groups = u_s.reshape(n_groups, ns, d_ff2)
    conv_s = u_groups[:Bs, ns - (CONV_W - 1):]
    left_meta = u_groups[n_groups - 1:, ns - (CONV_W - 1):]

    meta_ops = {k: sm_b[k][n_groups - 1:] for k in ('kfa', 'vf', 'kc', 'vm')}
    pos_p = n_meta + jnp.arange(S)
    pr = _project(x_prompt, _rope_tables(pos_p), sm['ftot'], wts, _lower_tri(MXU_DIM, MXU_DIM), ts=PROMPT_ROWS,
                  cs=MXU_DIM, cache_row0=n_meta)
    of_p, om_p = _attention(pr, meta_ops, tq=PROMPT_TQ, tk=PROMPT_TQ, tkp=n_meta, pps=PROMPT_PAIRS)
    y_prompt, conv_p = _merge_ffn(x_prompt, of_p, om_p, pr['gate'], left_meta, wts, ts=PROMPT_ROWS)
    caches_p = _fill_rows([pr[k] for k in CACHE_KEYS], [sm_b[k][n_groups - 1] for k in CACHE_KEYS])
    caches_p = [c.reshape((1, B, n_meta + S) + tail)
                for c, tail in zip(caches_p, ((FOX_HEADS, FOX_HEAD_DIM),) * 2 + tuple((c.shape[3],) for c in caches_p[2:]))]

    def sample_rows(name, tail):
        return sm_b[name][:Bs].reshape((1, Bs, ns) + tail)

    hd = (FOX_HEADS, FOX_HEAD_DIM)
    return (y_prompt, y_sample, *caches_p, conv_p[None],
            sample_rows('fk', hd), sample_rows('fv', hd), sample_rows('logf', (FOX_HEADS,)),
            sample_rows('ckv', (cache_mla_ckv.shape[-1],)), sample_rows('kr', (MLA_ROPE,)), conv_s[None])
```

```python
import functools
import math

import jax
import jax.numpy as jnp
import numpy as np
from jax import lax
from jax.experimental import pallas as pl
from jax.experimental.pallas import tpu as pltpu

F32 = jnp.float32
BF16 = jnp.bfloat16

N_META = 16
CHUNK = 64
EPS = 1e-6
NEG = -1e30
FOX_HEADS = 8
FOX_HEAD_DIM = 64
MLA_HEADS = 8
MLA_NOPE = 64
MLA_ROPE = 32
MLA_V = 64
MLA_QK = MLA_NOPE + MLA_ROPE
ROPE_BASE = 10000.0
CONV_W = 3

FOX_W = FOX_HEADS * FOX_HEAD_DIM
LOG2E = math.log2(math.e)
FOX_QSCALE = LOG2E / math.sqrt(FOX_HEAD_DIM)
MLA_QSCALE = LOG2E / math.sqrt(MLA_QK)

LANES = 128
MXU_DIM = 256
VMEM_BYTES_V7X = 64 * 1024 * 1024
F_PARTS = 3
HEADS_PER_PAIR = LANES // FOX_HEAD_DIM
N_PAIRS = FOX_HEADS // HEADS_PER_PAIR
PAIR_W = 2 * LANES
HALO = 8
GATE_CHUNKS = 4
UP_AHEAD = 3

C_FQ = 0
C_FK = C_FQ + FOX_W
C_FV = C_FK + FOX_W
C_FF = C_FV + FOX_W
C_CQ = C_FF + LANES


def _rms(x, g):
    return x * lax.rsqrt(jnp.mean(x * x, axis=-1, keepdims=True) + EPS) * g


def _split3(x):
    hi = x.astype(BF16)
    r = x - hi.astype(F32)
    mid = r.astype(BF16)
    lo = (r - mid.astype(F32)).astype(BF16)
    return hi, mid, lo


def _cumsum_rows(lf, tri, carry, cs):
    hi, mid, lo = _split3(lf)
    pieces = []
    for c in range(lf.shape[0] // cs):
        sl = slice(c * cs, (c + 1) * cs)
        fc = (jnp.dot(tri, hi[sl], preferred_element_type=F32)
              + jnp.dot(tri, mid[sl], preferred_element_type=F32)
              + jnp.dot(tri, lo[sl], preferred_element_type=F32)) + carry
        carry = fc[cs - 1:cs, :]
        pieces.append(fc)
    f = pieces[0] if len(pieces) == 1 else jnp.concatenate(pieces, axis=0)
    return f, carry


def _forget_columns(f):
    hi, mid, lo = _split3(f)
    lane = lax.broadcasted_iota(jnp.int32, f.shape, 1)
    zero = jnp.zeros_like(hi)
    return jnp.where(lane < FOX_HEADS, hi,
                     jnp.where(lane < 2 * FOX_HEADS, mid,
                               jnp.where(lane < F_PARTS * FOX_HEADS, lo, zero)))


def _store_pairs(ref, main, extra):
    for j in range(N_PAIRS):
        ref[0, :, PAIR_W * j:PAIR_W * j + LANES] = main[:, LANES * j:LANES * (j + 1)]
        ref[0, :, PAIR_W * j + LANES:PAIR_W * (j + 1)] = extra


def _store_cache(ref, val):
    lead = (0,) * (len(ref.shape) - 2)
    rows, width = val.shape
    if ref.shape[-1] == width:
        ref[lead] = val
    else:
        dh = ref.shape[-1]
        heads = width // dh
        assert ref.shape[-2] == rows * heads
        for h in range(heads):
            ref[lead + (pl.ds(h, rows, stride=heads), slice(None))] = val[:, h * dh:(h + 1) * dh]


def _rope(x, ct, sa, sb):
    return x * ct + pltpu.roll(x, LANES - MLA_ROPE // 2, 1) * sa + pltpu.roll(x, MLA_ROPE // 2, 1) * sb


def _proj_kernel(x_ref, ct_ref, sa_ref, sb_ref, finit_ref, gmix_ref, w_ref, bexp_ref, gq_ref, wuq_ref,
                 gkv_ref, wukv_ref, tri_ref,
                 fk_ref, fv_ref, logf_ref, ckv_ref, kr_ref, qf_ref, kfa_ref, vf_ref, qn_ref, qr_ref,
                 kc_ref, vm_ref, gate_ref, ftot_ref, carry_ref, *, cs, q_lora, kv_lora, d_model):
    t = pl.program_id(1)

    @pl.when(t == 0)
    def _():
        carry_ref[...] = finit_ref[0]

    c_ckv = C_CQ + q_lora
    c_kr = c_ckv + kv_lora
    c_gate = c_kr + LANES
    xb = _rms(x_ref[0], gmix_ref[...]).astype(BF16)

    def proj(a, b):
        return jnp.dot(xb, w_ref[:, a:b], preferred_element_type=F32)

    ct, sa, sb = ct_ref[...], sa_ref[...], sb_ref[...]
    nope_w = MLA_HEADS * MLA_NOPE
    cq = proj(C_CQ, c_ckv)
    ckv = proj(c_ckv, c_kr)
    ff = proj(C_FF, C_CQ)
    kr = proj(c_kr, c_gate)

    qf_ref[0] = (proj(C_FQ, C_FK) * FOX_QSCALE).T.astype(BF16)
    fk = proj(C_FK, C_FV)
    _store_cache(fk_ref, fk)
    fv = proj(C_FV, C_FF)
    _store_cache(fv_ref, fv)
    vf_ref[0] = fv.T.astype(BF16)

    cqn = _rms(cq, gq_ref[...]).astype(BF16)
    ckvn = _rms(ckv, gkv_ref[...])
    _store_cache(ckv_ref, ckvn)
    q = jnp.dot(cqn, wuq_ref[...], preferred_element_type=F32) * MLA_QSCALE
    kv = jnp.dot(ckvn.astype(BF16), wukv_ref[...], preferred_element_type=F32)

    lf = jax.nn.log_sigmoid(ff + bexp_ref[...])
    _store_cache(logf_ref, lf[:, :FOX_HEADS])
    f, carry = _cumsum_rows(lf, tri_ref[...], carry_ref[0:1, :], cs)
    carry_ref[0:1, :] = carry
    ftot_ref[0] = jnp.broadcast_to(carry, ftot_ref.shape[1:])

    gw = 2 * d_model // GATE_CHUNKS
    for c in range(GATE_CHUNKS):
        gate_ref[0, :, c * gw:(c + 1) * gw] = jax.nn.sigmoid(proj(c_gate + c * gw, c_gate + (c + 1) * gw))

    _store_pairs(kfa_ref, fk.astype(BF16), _forget_columns(f * LOG2E))
    qn_ref[0] = q[:, :nope_w].T.astype(BF16)
    for c in range(MLA_HEADS * MLA_ROPE // LANES):
        qr_ref[0, LANES * c:LANES * (c + 1), :] = _rope(
            q[:, nope_w + LANES * c:nope_w + LANES * (c + 1)], ct, sa, sb).T.astype(BF16)
    vm_ref[0] = kv[:, nope_w:].T.astype(BF16)
    kr4 = _rope(kr, ct, sa, sb)
    _store_cache(kr_ref, kr4[:, :MLA_ROPE])
    _store_pairs(kc_ref, kv[:, :nope_w].astype(BF16), kr4.astype(BF16))


TRANSPOSED = ('qf', 'vf', 'qn', 'qr', 'vm')
CACHE_KEYS = ('fk', 'fv', 'logf', 'ckv', 'kr')


def _const_spec(shape):
    return pl.BlockSpec(shape, lambda *_: (0,) * len(shape), pipeline_mode=pl.Buffered(1))


def _vmem_limit(block_bytes, const_bytes, temp_bytes):
    need = 2 * block_bytes + const_bytes + temp_bytes
    assert need < VMEM_BYTES_V7X, need
    return int(need)


def _project(x, tables, finit, wts, tri, *, ts, cs, cache_row0=None):
    B, n, d = x.shape
    assert n % ts == 0 and ts % cs == 0
    nt = n // ts
    q_lora = wts['g_q'].shape[1]
    kv_lora = wts['g_kv'].shape[1]
    ct, sa, sb = tables
    row = lambda w: pl.BlockSpec((1, ts, w), lambda b, t: (b, t, 0))
    tab = pl.BlockSpec((ts, LANES), lambda b, t: (t, 0))
    in_specs = [row(d), tab, tab, tab,
                pl.BlockSpec((1, 8, LANES), lambda b, t: (0, 0, 0)),
                _const_spec(wts['g_mix'].shape), _const_spec(wts['w_a'].shape), _const_spec(wts['b_exp'].shape),
                _const_spec(wts['g_q'].shape), _const_spec(wts['w_uq'].shape), _const_spec(wts['g_kv'].shape),
                _const_spec(wts['w_ukv'].shape), _const_spec(tri.shape)]
    widths = dict(fk=(FOX_W, F32), fv=(FOX_W, F32), logf=(FOX_HEADS, F32), ckv=(kv_lora, F32), kr=(MLA_ROPE, F32),
                  qf=(FOX_W, BF16), kfa=(N_PAIRS * PAIR_W, BF16), vf=(FOX_W, BF16),
                  qn=(MLA_HEADS * MLA_NOPE, BF16), qr=(MLA_HEADS * MLA_ROPE, BF16),
                  kc=(N_PAIRS * PAIR_W, BF16), vm=(MLA_HEADS * MLA_V, BF16), gate=(2 * d, F32))
    col = lambda w: pl.BlockSpec((1, w, ts), lambda b, t: (b, 0, t))
    out_shape = [jax.ShapeDtypeStruct((B, w, n) if k in TRANSPOSED else (B, n, w), dt) for k, (w, dt) in widths.items()]
    out_specs = [col(w) if k in TRANSPOSED else row(w) for k, (w, _) in widths.items()]
    if cache_row0 is not None:
        assert cache_row0 % 8 == 0 and ts % 8 == 0
        for idx, k in enumerate(CACHE_KEYS):
            heads, w = (FOX_HEADS, FOX_HEAD_DIM) if k in ('fk', 'fv') else (1, widths[k][0])
            out_shape[idx] = jax.ShapeDtypeStruct((1, B, (cache_row0 + n) * heads, w), F32)
            out_specs[idx] = pl.BlockSpec(
                tuple(pl.Element(s) for s in (1, 1, ts * heads, w)),
                lambda b, t, heads=heads: (0, b, pl.multiple_of((cache_row0 + t * ts) * heads, 8), 0))
    out_shape.append(jax.ShapeDtypeStruct((B, 8, LANES), F32))
    out_specs.append(pl.BlockSpec((1, 8, LANES), lambda b, t: (b, 0, 0)))
    block_bytes = ts * (d * 4 + 3 * LANES * 4 + sum(w * jnp.dtype(dt).itemsize for w, dt in widths.values()))
    const_bytes = sum(int(np.prod(wts[k].shape)) * wts[k].dtype.itemsize for k in ('w_a', 'w_uq', 'w_ukv')) + tri.size * 2
    temp_bytes = ts * (2 * d + 2 * d) * 4 * 2
    outs = pl.pallas_call(
        functools.partial(_proj_kernel, cs=cs, q_lora=q_lora, kv_lora=kv_lora, d_model=d),
        grid=(B, nt), in_specs=in_specs, out_specs=out_specs, out_shape=out_shape,
        scratch_shapes=[pltpu.VMEM((8, LANES), F32)],
        compiler_params=pltpu.CompilerParams(
            dimension_semantics=("arbitrary", "arbitrary"),
            vmem_limit_bytes=_vmem_limit(block_bytes, const_bytes, temp_bytes)),
        name="proj",
    )(x, ct, sa, sb, finit, wts['g_mix'], wts['w_a'], wts['b_exp'], wts['g_q'], wts['w_uq'], wts['g_kv'],
      wts['w_ukv'], tri)
    res = dict(zip(widths.keys(), outs[:-1]))
    res['ftot'] = outs[-1]
    return res


def _fill_kernel(*refs):
    n = len(refs) // 3
    for src, dst in zip(refs[:n], refs[2 * n:]):
        _store_cache(dst, src[...])


def _fill_rows(caches, head_rows):
    n = len(caches)
    B = caches[0].shape[1]
    dst_specs = [pl.BlockSpec((1, 1, h.shape[0] * h.shape[1] // c.shape[3], c.shape[3]), lambda b: (0, b, 0, 0))
                 for c, h in zip(caches, head_rows)]
    outs = pl.pallas_call(
        _fill_kernel, grid=(B,),
        in_specs=[_const_spec(h.shape) for h in head_rows] + [pl.BlockSpec(memory_space=pl.ANY)] * n,
        out_specs=dst_specs,
        out_shape=[jax.ShapeDtypeStruct(c.shape, c.dtype) for c in caches],
        input_output_aliases={n + k: k for k in range(n)},
        compiler_params=pltpu.CompilerParams(dimension_semantics=("arbitrary",)),
        name="fill",
    )(*head_rows, *caches)
    return outs


def _load_heads(ref, rows, heads):
    return jnp.concatenate([ref[0, pl.ds(h, rows, stride=heads), :] for h in range(heads)], axis=1)


def _prefix_kernel(ck_ref, cv_ref, lf_ref, ckv_ref, kr_ref, wukv_ref, tri_ref,
                   kfa_ref, vf_ref, kc_ref, vm_ref, *, cs):
    P = lf_ref.shape[1]
    lf = lf_ref[0]
    lf_exp = jnp.concatenate([lf] * F_PARTS + [jnp.zeros((P, LANES - F_PARTS * FOX_HEADS), F32)], axis=1)
    f, total = _cumsum_rows(lf_exp, tri_ref[...], jnp.zeros((1, LANES), F32), cs)
    ck = _load_heads(ck_ref, P, FOX_HEADS)
    _store_pairs(kfa_ref, ck.astype(BF16), _forget_columns((f - total) * LOG2E))
    vf_ref[0] = _load_heads(cv_ref, P, FOX_HEADS).T.astype(BF16)
    kv = jnp.dot(ckv_ref[0].astype(BF16), wukv_ref[...], preferred_element_type=F32)
    nope_w = MLA_HEADS * MLA_NOPE
    vm_ref[0] = kv[:, nope_w:].T.astype(BF16)
    kr4 = jnp.concatenate([kr_ref[0]] * (LANES // MLA_ROPE), axis=1)
    _store_pairs(kc_ref, kv[:, :nope_w].astype(BF16), kr4.astype(BF16))


def _prefix_operands(ck, cv, logf, cckv, kr, w_ukv, tri, *, cs):
    B, P, _ = logf.shape
    row = lambda r, w: pl.BlockSpec((1, r, w), lambda b: (b, 0, 0))
    widths = dict(kfa=N_PAIRS * PAIR_W, vf=FOX_W, kc=N_PAIRS * PAIR_W, vm=MLA_HEADS * MLA_V)
    ins = (ck, cv, logf, cckv, kr)
    block_bytes = sum(a.shape[1] * max(a.shape[2], LANES) * 4 for a in ins) + P * sum(widths.values()) * 2
    outs = pl.pallas_call(
        functools.partial(_prefix_kernel, cs=cs),
        grid=(B,),
        in_specs=[row(a.shape[1], a.shape[2]) for a in ins] + [_const_spec(w_ukv.shape), _const_spec(tri.shape)],
        out_specs=[row(w, P) if k in TRANSPOSED else row(P, w) for k, w in widths.items()],
        out_shape=[jax.ShapeDtypeStruct((B, w, P) if k in TRANSPOSED else (B, P, w), BF16) for k, w in widths.items()],
        compiler_params=pltpu.CompilerParams(
            dimension_semantics=("arbitrary",),
            vmem_limit_bytes=_vmem_limit(block_bytes, w_ukv.size * 2 + tri.size * 2, P * 1024 * 4 * 4)),
        name="prefix",
    )(*ins, w_ukv, tri)
    return dict(zip(widths.keys(), outs))


def _scores(ks, qts):
    return tuple(jnp.dot(k, qt, preferred_element_type=F32) for k, qt in zip(ks, qts))


def _online_update(states, scores, vts, masks=None, extra=None):
    mids = []
    for c, ((m, l, acc), s) in enumerate(zip(states, scores)):
        if masks is not None:
            s = jnp.where(masks[c], s, NEG)
        n_extra = 0
        if extra is not None:
            n_extra = extra[0][c].shape[0]
            s = jnp.concatenate([extra[0][c], s], axis=0)
        m_new = jnp.maximum(m, jnp.max(s, axis=0, keepdims=True))
        alpha = jnp.exp2(m - m_new)
        p = jnp.exp2(s - m_new)
        mids.append((m_new, alpha * l + jnp.sum(p, axis=0, keepdims=True), alpha * acc, p.astype(BF16), n_extra))
    out = []
    for c, ((m_new, l, acc, p, n_extra), vt) in enumerate(zip(mids, vts)):
        if n_extra:
            acc = acc + jnp.dot(extra[1][c], p[:n_extra], preferred_element_type=F32)
        out.append((m_new, l, acc + jnp.dot(vt, p[n_extra:], preferred_element_type=F32)))
    return tuple(out)


def _attn_kernel(*refs, tq, tk, tkp, nq, pps, has_prefix, n_pref, transpose_out):
    if has_prefix:
        (qf_ref, qn_ref, qr_ref, kfa_ref, vf_ref, kc_ref, vm_ref,
         pkfa_ref, pvf_ref, pkc_ref, pvm_ref, of_ref, om_ref, sa_ref, sb_ref) = refs
    else:
        qf_ref, qn_ref, qr_ref, kfa_ref, vf_ref, kc_ref, vm_ref, of_ref, om_ref, sa_ref, sb_ref = refs
        pkfa_ref = pvf_ref = pkc_ref = pvm_ref = None
    b, j, i = pl.program_id(0), pl.program_id(1), pl.program_id(2)
    pref_bias = None if (not has_prefix or n_pref is None) else jnp.where(b < n_pref, 0.0, NEG)

    frow = lax.broadcasted_iota(jnp.int32, (LANES, tq), 0)
    zero_half = jnp.zeros((FOX_HEAD_DIM, tq), BF16)
    rope_groups = LANES // MLA_ROPE
    rope_rows_per_step = pps * HEADS_PER_PAIR * MLA_ROPE
    chains = []
    for mixer in range(2):
        for p in range(pps):
            for s in range(HEADS_PER_PAIR):
                h = HEADS_PER_PAIR * (pps * j + p) + s
                vrows = slice(p * LANES + s * FOX_HEAD_DIM, p * LANES + (s + 1) * FOX_HEAD_DIM)
                own = [zero_half] * HEADS_PER_PAIR
                own[s] = (qf_ref if mixer == 0 else qn_ref)[0, vrows, :]
                if mixer == 0:
                    extra = jnp.where((frow < F_PARTS * FOX_HEADS) & ((frow & (FOX_HEADS - 1)) == h),
                                      -1.0, 0.0).astype(BF16)
                else:
                    r0 = (p * HEADS_PER_PAIR * MLA_ROPE // LANES) * LANES if rope_rows_per_step > LANES else 0
                    qr = qr_ref[0, r0:r0 + LANES, :]
                    extra = jnp.where((frow >> int(math.log2(MLA_ROPE))) == (h & (rope_groups - 1)), qr, jnp.zeros_like(qr))
                chains.append((jnp.concatenate(own + [extra], axis=0), slice(p * PAIR_W, (p + 1) * PAIR_W), vrows, mixer))
    qts = [c[0] for c in chains]
    key_refs, val_refs = (kfa_ref, kc_ref), (vf_ref, vm_ref)
    pkey_refs, pval_refs = (pkfa_ref, pkc_ref), (pvf_ref, pvm_ref)

    def scores(krefs, rows):
        return _scores([krefs[c[3]][0, rows, c[1]] for c in chains], qts)

    def update(sts, sc, vrefs, rows, **kw):
        return _online_update(sts, sc, [vrefs[c[3]][0, c[2], rows] for c in chains], **kw)

    assert tq == tk
    tile_rows = (lambda t: pl.ds(pl.multiple_of(t * tk, tk), tk)) if nq > 1 else (lambda t: slice(0, tk))

    def put_scores(ref, t):
        for c, s in enumerate(scores(key_refs, tile_rows(t))):
            ref[c] = s

    def step(sts, ref, t, **kw):
        return update(sts, [ref[c] for c in range(len(chains))], val_refs, tile_rows(t), **kw)

    prefix = None
    if has_prefix:
        assert tkp == pkfa_ref.shape[1]
        all_rows = slice(0, tkp)
        psc = scores(pkey_refs, all_rows)
        if pref_bias is not None:
            psc = [s + pref_bias for s in psc]
        prefix = (psc, [pval_refs[c[3]][0, c[2], all_rows] for c in chains])
    put_scores(sa_ref, 0)

    def finish(sts, ref):
        krow = lax.broadcasted_iota(jnp.int32, (tq, tq), 0)
        qcol = lax.broadcasted_iota(jnp.int32, (tq, tq), 1)
        chunk_shift = int(math.log2(CHUNK))
        masks = (krow <= qcol, (krow >> chunk_shift) <= (qcol >> chunk_shift))
        sts = step(sts, ref, i, masks=[masks[c[3]] for c in chains], extra=prefix)
        outs = [acc / l for _, l, acc in sts]
        half = len(outs) // 2
        for o_ref, group in ((of_ref, outs[:half]), (om_ref, outs[half:])):
            ot = jnp.concatenate(group, axis=0)
            o_ref[0] = (ot.T if transpose_out else ot).astype(BF16)

    init = (jnp.full((1, tq), NEG, F32), jnp.zeros((1, tq), F32), jnp.zeros((FOX_HEAD_DIM, tq), F32))
    states = tuple(init for _ in chains)
    if nq == 1:
        finish(states, sa_ref)
    else:
        def body(u, sts):
            put_scores(sb_ref, 2 * u + 1)
            sts = step(sts, sa_ref, 2 * u)
            put_scores(sa_ref, 2 * u + 2)
            return step(sts, sb_ref, 2 * u + 1)

        states = lax.fori_loop(0, i >> 1, body, states)

        @pl.when((i & 1) == 0)
        def _():
            finish(states, sa_ref)

        @pl.when((i & 1) == 1)
        def _():
            put_scores(sb_ref, i)
            finish(step(states, sa_ref, i - 1), sb_ref)


def _attention(ops, prefix, *, tq, tk, tkp, pps, n_pref=None):
    B, n, _ = ops['kfa'].shape
    assert n % tq == 0 and tq % tk == 0 and N_PAIRS % pps == 0
    nq = n // tq
    transpose_out = tq % LANES == 0
    rope_w = max(LANES, pps * HEADS_PER_PAIR * MLA_ROPE)
    steps_per_rope_block = rope_w // (pps * HEADS_PER_PAIR * MLA_ROPE)
    qspec = pl.BlockSpec((1, pps * LANES, tq), lambda b, j, i: (b, j, i))
    qrspec = pl.BlockSpec((1, rope_w, tq), lambda b, j, i: (b, j // steps_per_rope_block, i))
    kspec = pl.BlockSpec((1, n, pps * PAIR_W), lambda b, j, i: (b, 0, j))
    vspec = pl.BlockSpec((1, pps * LANES, n), lambda b, j, i: (b, j, 0))
    args = [ops['qf'], ops['qn'], ops['qr'], ops['kfa'], ops['vf'], ops['kc'], ops['vm']]
    in_specs = [qspec, qspec, qrspec, kspec, vspec, kspec, vspec]
    block_bytes = 2 * pps * (3 * tq * LANES + 2 * n * (PAIR_W + LANES) + 2 * tq * LANES)
    if prefix is not None:
        Bp, P, _ = prefix['kfa'].shape
        assert P % tkp == 0
        if Bp == 1:
            pb = lambda b: 0
        elif n_pref is not None:
            pb = lambda b: jnp.minimum(b, n_pref - 1)
        else:
            pb = lambda b: b
        pk = pl.BlockSpec((1, P, pps * PAIR_W), lambda b, j, i: (pb(b), 0, j))
        pv = pl.BlockSpec((1, pps * LANES, P), lambda b, j, i: (pb(b), j, 0))
        args += [prefix['kfa'], prefix['vf'], prefix['kc'], prefix['vm']]
        in_specs += [pk, pv, pk, pv]
        block_bytes += 2 * 2 * pps * P * (PAIR_W + LANES)
    if transpose_out:
        ospec = pl.BlockSpec((1, tq, pps * LANES), lambda b, j, i: (b, i, j))
        oshape = (B, n, N_PAIRS * LANES)
    else:
        ospec = pl.BlockSpec((1, pps * LANES, tq), lambda b, j, i: (b, j, i))
        oshape = (B, N_PAIRS * LANES, n)
    n_chains = 2 * HEADS_PER_PAIR * pps
    tq_pad = max(tq, LANES)
    temp_bytes = n_chains * 6 * tq_pad * max(tk, tkp if prefix is not None else tk) * 4 + 4 * n_chains * tq_pad * LANES * 4
    of, om = pl.pallas_call(
        functools.partial(_attn_kernel, tq=tq, tk=tk, tkp=tkp, has_prefix=prefix is not None, n_pref=n_pref,
                          nq=nq, pps=pps, transpose_out=transpose_out),
        grid=(B, N_PAIRS // pps, nq), in_specs=in_specs, out_specs=[ospec, ospec],
        out_shape=[jax.ShapeDtypeStruct(oshape, BF16), jax.ShapeDtypeStruct(oshape, BF16)],
        scratch_shapes=[pltpu.VMEM((n_chains, tk, tq), F32), pltpu.VMEM((n_chains, tk, tq), F32)],
        compiler_params=pltpu.CompilerParams(
            dimension_semantics=("arbitrary", "arbitrary", "arbitrary"),
            vmem_limit_bytes=_vmem_limit(block_bytes, 2 * n_chains * tk * tq * 4, temp_bytes)),
        name="attn",
    )(*args)
    if not transpose_out:
        of, om = jnp.swapaxes(of, 1, 2), jnp.swapaxes(om, 1, 2)
    return of, om


def _ffn_kernel(*refs, d_ff, fc, flat_group):
    if flat_group:
        (h_ref, of_ref, om_ref, g_ref, left_ref, wof_ref, wom_ref, wout_ref, gffn_ref, wup_ref,
         cw_ref, cb_ref, wdown_ref, gfin_ref, y_ref, u_ref, l1_ref, l2_ref, _) = refs
        l1_ref[...] = jnp.zeros_like(l1_ref)
        l2_ref[...] = jnp.zeros_like(l2_ref)
        for g in range(left_ref.shape[0]):
            r0 = g * flat_group
            l1_ref[r0:r0 + 1, :] = left_ref[g, 1:2, :]
            l2_ref[r0:r0 + 1, :] = left_ref[g, 0:1, :]
            l2_ref[r0 + 1:r0 + 2, :] = left_ref[g, 1:2, :]
        ub_ref = refs[-1]
        ub_ref[0:HALO, :] = jnp.zeros((HALO, ub_ref.shape[1]), F32)
    else:
        (h_ref, of_ref, om_ref, g_ref, left_ref, wof_ref, wom_ref, wout_ref, gffn_ref, wup_ref,
         cw_ref, cb_ref, wdown_ref, gfin_ref, y_ref, u_ref, ub_ref) = refs

        @pl.when(pl.program_id(1) == 0)
        def _():
            ub_ref[HALO - (CONV_W - 1):HALO, :] = left_ref[0]

    ts, d = h_ref.shape[1], h_ref.shape[2]
    ya = jnp.dot(of_ref[0], wof_ref[...], preferred_element_type=F32)
    yb = jnp.dot(om_ref[0], wom_ref[...], preferred_element_type=F32)
    mix = g_ref[0, :, :d] * ya + g_ref[0, :, d:] * yb
    h1 = h_ref[0] + jnp.dot(mix.astype(BF16), wout_ref[...], preferred_element_type=F32)
    xn = _rms(h1, gffn_ref[...]).astype(BF16)

    if flat_group:
        row = lax.broadcasted_iota(jnp.int32, (ts, 1), 0)
        keep1 = ((row & (flat_group - 1)) >= 1).astype(F32)
        keep2 = ((row & (flat_group - 1)) >= 2).astype(F32)

    def up(c0):
        return jnp.dot(xn, wup_ref[:, c0:c0 + fc], preferred_element_type=F32)

    def conv(u, c0):
        cols = slice(c0, c0 + fc)
        ub_ref[HALO:HALO + ts, cols] = u
        u1 = ub_ref[HALO - 1:HALO - 1 + ts, cols]
        u2 = ub_ref[HALO - 2:HALO - 2 + ts, cols]
        if flat_group:
            u_ref[0, :, cols] = u
            u1 = u1 * keep1 + l1_ref[:, cols]
            u2 = u2 * keep2 + l2_ref[:, cols]
        return cb_ref[:, cols] + u2 * cw_ref[0:1, cols] + u1 * cw_ref[1:2, cols] + u * cw_ref[2:3, cols]

    n_chunks = d_ff // fc
    acc = jnp.zeros((ts, d), F32)
    ups = [(up(c * fc), up(d_ff + c * fc)) for c in range(min(UP_AHEAD, n_chunks))]
    for c in range(n_chunks):
        u_gate, u_val = ups.pop(0)
        if c + UP_AHEAD < n_chunks:
            ups.append((up((c + UP_AHEAD) * fc), up(d_ff + (c + UP_AHEAD) * fc)))
        act = (jax.nn.silu(conv(u_gate, c * fc)) * conv(u_val, d_ff + c * fc)).astype(BF16)
        acc = acc + jnp.dot(act, wdown_ref[c * fc:(c + 1) * fc, :], preferred_element_type=F32)
    y_ref[0] = _rms(h1 + acc, gfin_ref[...])
    if not flat_group:
        tail = ub_ref[HALO + ts - (CONV_W - 1):HALO + ts, :]
        u_ref[0] = tail
        ub_ref[HALO - (CONV_W - 1):HALO, :] = tail


def _merge_ffn(h, of, om, gates, left, wts, *, ts, flat_group=0):
    B, n, d = h.shape
    d_ff = wts['w_down'].shape[0]
    fc = MXU_DIM
    assert n % ts == 0 and d_ff % fc == 0
    row = lambda w: pl.BlockSpec((1, ts, w), lambda b, t: (b, t, 0))
    consts = [wts[k] for k in ('w_o_fox', 'w_o_mla', 'w_out', 'g_ffn', 'w_up', 'conv_w', 'conv_b', 'w_down', 'g_fin')]
    const_specs = [_const_spec(c.shape) for c in consts]
    in_specs = [row(d), row(of.shape[2]), row(om.shape[2]), row(2 * d)]
    args = [h, of, om, gates, left]
    if flat_group:
        assert B == 1 and ts % flat_group == 0 and left.shape[0] * flat_group == n
        gpt = ts // flat_group
        in_specs.append(pl.BlockSpec((gpt, CONV_W - 1, 2 * d_ff), lambda b, t: (t, 0, 0)))
        u_shape, u_spec = (1, n, 2 * d_ff), row(2 * d_ff)
        scratch = [pltpu.VMEM((ts, 2 * d_ff), F32), pltpu.VMEM((ts, 2 * d_ff), F32)]
        const_extra = 2 * ts * 2 * d_ff * 4 + 2 * gpt * 8 * 2 * d_ff * 4
    else:
        bl = left.shape[0]
        in_specs.append(pl.BlockSpec((1, CONV_W - 1, 2 * d_ff), lambda b, t: (b if bl > 1 else 0, 0, 0)))
        u_shape = (B, CONV_W - 1, 2 * d_ff)
        u_spec = pl.BlockSpec((1, CONV_W - 1, 2 * d_ff), lambda b, t: (b, 0, 0))
        scratch = []
        const_extra = 0
    scratch.append(pltpu.VMEM((HALO + ts, 2 * d_ff), F32))
    const_extra += (HALO + ts) * 2 * d_ff * 4
    u_rows = ts if flat_group else 8
    block_bytes = ts * (d * 4 + (of.shape[2] + om.shape[2]) * 2 + 2 * d * 4 + d * 4) + u_rows * 2 * d_ff * 4
    const_bytes = sum(c.size * c.dtype.itemsize for c in consts) + const_extra
    temp_bytes = ts * d * 4 * 8 + ts * fc * 4 * 16
    y, u = pl.pallas_call(
        functools.partial(_ffn_kernel, d_ff=d_ff, fc=fc, flat_group=flat_group),
        grid=(B, n // ts), in_specs=in_specs + const_specs,
        out_specs=[row(d), u_spec],
        out_shape=[jax.ShapeDtypeStruct((B, n, d), F32), jax.ShapeDtypeStruct(u_shape, F32)],
        scratch_shapes=scratch,
        compiler_params=pltpu.CompilerParams(
            dimension_semantics=("arbitrary", "arbitrary"),
            vmem_limit_bytes=_vmem_limit(block_bytes, const_bytes, temp_bytes)),
        name="ffn",
    )(*args, *consts)
    return y, u


def _rope_tables(pos):
    half = MLA_ROPE // 2
    inv = ROPE_BASE ** (-jnp.arange(0, MLA_ROPE, 2, dtype=F32) / MLA_ROPE)
    ang = pos.astype(F32)[:, None] * inv[None, :]
    cos, sin = jnp.cos(ang), jnp.sin(ang)
    zero = jnp.zeros_like(sin)
    reps = LANES // MLA_ROPE
    ct = jnp.tile(jnp.concatenate([cos, cos], axis=1), (1, reps))
    sa = jnp.tile(jnp.concatenate([-sin, zero], axis=1), (1, reps))
    sb = jnp.tile(jnp.concatenate([zero, sin], axis=1), (1, reps))
    assert half * 2 == MLA_ROPE
    return ct, sa, sb


def _lower_tri(n, group):
    i = np.arange(n)
    m = (i[None, :] <= i[:, None]) & ((i[None, :] // group) == (i[:, None] // group))
    return jnp.asarray(m, BF16)


def _prepare_weights(norm_mix_g, w_in, b_forget, mla_q_norm_g, w_uq, mla_kv_norm_g, w_ukv, w_o_fox, w_o_mla, w_out,
                     norm_ffn_g, w_up, conv_w, conv_b, w_down, norm_final_g):
    d = w_in.shape[0]
    q_lora, kv_lora = mla_q_norm_g.shape[0], mla_kv_norm_g.shape[0]
    off_ff = 3 * FOX_W
    off_cq = off_ff + FOX_HEADS
    off_ckv = off_cq + q_lora
    off_kr = off_ckv + kv_lora
    off_gate = off_kr + MLA_ROPE
    pad = LANES - F_PARTS * FOX_HEADS
    w_ff = jnp.pad(jnp.tile(w_in[:, off_ff:off_cq], (1, F_PARTS)), ((0, 0), (0, pad)))
    w_a = jnp.concatenate([
        w_in[:, :off_ff], w_ff, w_in[:, off_cq:off_kr],
        jnp.tile(w_in[:, off_kr:off_gate], (1, LANES // MLA_ROPE)),
        w_in[:, off_gate:]], axis=1).astype(BF16)
    uq = w_uq.reshape(q_lora, MLA_HEADS, MLA_QK)
    w_uq_p = jnp.concatenate([uq[:, :, :MLA_NOPE].reshape(q_lora, -1), uq[:, :, MLA_NOPE:].reshape(q_lora, -1)], axis=1)
    ukv = w_ukv.reshape(kv_lora, MLA_HEADS, MLA_NOPE + MLA_V)
    w_ukv_p = jnp.concatenate([ukv[:, :, :MLA_NOPE].reshape(kv_lora, -1), ukv[:, :, MLA_NOPE:].reshape(kv_lora, -1)], axis=1)
    return dict(
        g_mix=norm_mix_g.reshape(1, d), w_a=w_a,
        b_exp=jnp.pad(jnp.tile(b_forget, F_PARTS), (0, pad)).reshape(1, LANES),
        g_q=mla_q_norm_g.reshape(1, q_lora), w_uq=w_uq_p.astype(BF16),
        g_kv=mla_kv_norm_g.reshape(1, kv_lora), w_ukv=w_ukv_p.astype(BF16),
        w_o_fox=w_o_fox.astype(BF16), w_o_mla=w_o_mla.astype(BF16), w_out=w_out.astype(BF16),
        g_ffn=norm_ffn_g.reshape(1, d), w_up=w_up.astype(BF16), conv_w=conv_w, conv_b=conv_b.reshape(1, -1),
        w_down=w_down.astype(BF16), g_fin=norm_final_g.reshape(1, d))


PROJ_ROWS = 512
FFN_ROWS = 256
PROMPT_TQ = 256
PROMPT_PAIRS = 2
SMALL_ROWS = LANES
SMALL_PAIRS = 2


def kernel(x_prompt, x_sample, cache_fox_k, cache_fox_v, cache_fox_logf, cache_mla_ckv, cache_mla_krope, state_ffn_conv, meta_tokens, norm_mix_g, w_in, b_forget, mla_q_norm_g, w_uq, mla_kv_norm_g, w_ukv, w_o_fox, w_o_mla, w_out, norm_ffn_g, w_up, conv_w, conv_b, w_down, norm_final_g):
    assert w_in.shape[0] == 1, "single-layer model"
    B, S, d = x_prompt.shape
    Bs, ns, _ = x_sample.shape
    P = cache_fox_k.shape[2]
    n_meta = meta_tokens.shape[0]
    assert n_meta == N_META == ns
    wts = _prepare_weights(norm_mix_g[0], w_in[0], b_forget[0], mla_q_norm_g[0], w_uq[0], mla_kv_norm_g[0], w_ukv[0],
                           w_o_fox[0], w_o_mla[0], w_out[0], norm_ffn_g[0], w_up[0], conv_w[0], conv_b[0], w_down[0],
                           norm_final_g)
    d_ff2 = w_up.shape[2]

    r_pad = -(-(Bs + 1) * ns // SMALL_ROWS) * SMALL_ROWS
    n_groups = r_pad // ns
    n_fill = r_pad - (Bs + 1) * ns
    rows = jnp.concatenate([x_sample.reshape(Bs * ns, d), jnp.zeros((n_fill, d), x_sample.dtype),
                            meta_tokens.astype(x_sample.dtype)], axis=0)[None]
    pos_small = jnp.concatenate([jnp.tile(P + jnp.arange(ns), Bs), jnp.zeros((n_fill,), jnp.int32), jnp.arange(n_meta)])
    zero_f = jnp.zeros((1, 8, LANES), F32)
    sm = _project(rows, _rope_tables(pos_small), zero_f, wts, _lower_tri(r_pad, ns), ts=r_pad, cs=r_pad)
    sm_b = {k: (jnp.swapaxes(v.reshape(v.shape[1], n_groups, ns), 0, 1) if k in TRANSPOSED
                else v.reshape(n_groups, ns, v.shape[-1])) for k, v in sm.items() if k != 'ftot'}
    live = lambda a: jnp.concatenate([a[:Bs], a[n_groups - 1:]], axis=0)

    prefix_s = _prefix_operands(
        cache_fox_k[0].reshape(Bs, P * FOX_HEADS, FOX_HEAD_DIM), cache_fox_v[0].reshape(Bs, P * FOX_HEADS, FOX_HEAD_DIM),
        cache_fox_logf[0], cache_mla_ckv[0], cache_mla_krope[0], wts['w_ukv'], _lower_tri(MXU_DIM, MXU_DIM), cs=MXU_DIM)
    of_s, om_s = _attention({k: live(v) for k, v in sm_b.items()}, prefix_s, tq=ns, tk=ns, tkp=P, pps=SMALL_PAIRS, n_pref=Bs)

    def spread(a):
        return jnp.concatenate([a[:Bs].reshape(Bs * ns, -1), jnp.zeros((n_fill, a.shape[-1]), a.dtype), a[Bs]], axis=0)[None]

    left_groups = jnp.concatenate([state_ffn_conv[0], jnp.zeros((n_groups - Bs, CONV_W - 1, d_ff2), F32)], axis=0)
    y_s, u_s = _merge_ffn(rows, spread(of_s), spread(om_s), sm['gate'], left_groups, wts, ts=SMALL_ROWS, flat_group=ns)
    y_sample = y_s[0, :Bs * ns].reshape(Bs, ns, d)
    u_groups = u_s.reshape(n_groups, ns, d_ff2)
    conv_s = u_groups[:Bs, ns - (CONV_W - 1):]
    left_meta = u_groups[n_groups - 1:, ns - (CONV_W - 1):]

    meta_ops = {k: sm_b[k][n_groups - 1:] for k in ('kfa', 'vf', 'kc', 'vm')}
    pos_p = n_meta + jnp.arange(S)
    pr = _project(x_prompt, _rope_tables(pos_p), sm['ftot'], wts, _lower_tri(MXU_DIM, MXU_DIM), ts=PROJ_ROWS,
                  cs=MXU_DIM, cache_row0=n_meta)
    of_p, om_p = _attention(pr, meta_ops, tq=PROMPT_TQ, tk=PROMPT_TQ, tkp=n_meta, pps=PROMPT_PAIRS)
    y_prompt, conv_p = _merge_ffn(x_prompt, of_p, om_p, pr['gate'], left_meta, wts, ts=FFN_ROWS)
    caches_p = _fill_rows([pr[k] for k in CACHE_KEYS], [sm_b[k][n_groups - 1] for k in CACHE_KEYS])
    caches_p = [c.reshape((1, B, n_meta + S) + tail)
                for c, tail in zip(caches_p, ((FOX_HEADS, FOX_HEAD_DIM),) * 2 + tuple((c.shape[3],) for c in caches_p[2:]))]

    def sample_rows(name, tail):
        return sm_b[name][:Bs].reshape((1, Bs, ns) + tail)

    hd = (FOX_HEADS, FOX_HEAD_DIM)
    return (y_prompt, y_sample, *caches_p, conv_p[None],
            sample_rows('fk', hd), sample_rows('fv', hd), sample_rows('logf', (FOX_HEADS,)),
            sample_rows('ckv', (cache_mla_ckv.shape[-1],)), sample_rows('kr', (MLA_ROPE,)), conv_s[None])
```

```python
import functools
import math

import jax
import jax.numpy as jnp
import numpy as np
from jax import lax
from jax.experimental import pallas as pl
from jax.experimental.pallas import tpu as pltpu

F32 = jnp.float32
BF16 = jnp.bfloat16

N_META = 16
CHUNK = 64
EPS = 1e-6
NEG = -1e30
FOX_HEADS = 8
FOX_HEAD_DIM = 64
MLA_HEADS = 8
MLA_NOPE = 64
MLA_ROPE = 32
MLA_V = 64
MLA_QK = MLA_NOPE + MLA_ROPE
ROPE_BASE = 10000.0
CONV_W = 3

FOX_W = FOX_HEADS * FOX_HEAD_DIM
LOG2E = math.log2(math.e)
FOX_QSCALE = LOG2E / math.sqrt(FOX_HEAD_DIM)
MLA_QSCALE = LOG2E / math.sqrt(MLA_QK)

LANES = 128
MXU_DIM = 256
VMEM_BYTES_V7X = 64 * 1024 * 1024
F_PARTS = 3
HEADS_PER_PAIR = LANES // FOX_HEAD_DIM
N_PAIRS = FOX_HEADS // HEADS_PER_PAIR
PAIR_W = 2 * LANES
HALO = 8
GATE_CHUNKS = 4
UP_AHEAD = 3

C_FQ = 0
C_FK = C_FQ + FOX_W
C_FV = C_FK + FOX_W
C_FF = C_FV + FOX_W
C_CQ = C_FF + LANES


def _rms(x, g):
    return x * lax.rsqrt(jnp.mean(x * x, axis=-1, keepdims=True) + EPS) * g


def _split3(x):
    hi = x.astype(BF16)
    r = x - hi.astype(F32)
    mid = r.astype(BF16)
    lo = (r - mid.astype(F32)).astype(BF16)
    return hi, mid, lo


def _cumsum_rows(lf, tri, carry, cs):
    hi, mid, lo = _split3(lf)
    pieces = []
    for c in range(lf.shape[0] // cs):
        sl = slice(c * cs, (c + 1) * cs)
        fc = (jnp.dot(tri, hi[sl], preferred_element_type=F32)
              + jnp.dot(tri, mid[sl], preferred_element_type=F32)
              + jnp.dot(tri, lo[sl], preferred_element_type=F32)) + carry
        carry = fc[cs - 1:cs, :]
        pieces.append(fc)
    f = pieces[0] if len(pieces) == 1 else jnp.concatenate(pieces, axis=0)
    return f, carry


def _forget_columns(f):
    hi, mid, lo = _split3(f)
    lane = lax.broadcasted_iota(jnp.int32, f.shape, 1)
    zero = jnp.zeros_like(hi)
    return jnp.where(lane < FOX_HEADS, hi,
                     jnp.where(lane < 2 * FOX_HEADS, mid,
                               jnp.where(lane < F_PARTS * FOX_HEADS, lo, zero)))


def _store_pairs(ref, main, extra):
    for j in range(N_PAIRS):
        ref[0, :, PAIR_W * j:PAIR_W * j + LANES] = main[:, LANES * j:LANES * (j + 1)]
        ref[0, :, PAIR_W * j + LANES:PAIR_W * (j + 1)] = extra


def _store_cache(ref, val, prev_ref=None):
    lead = (0,) * (len(ref.shape) - 2)
    width = prev_ref.shape[1] if val is None else val.shape[1]
    heads = width // ref.shape[-1]

    def put(row0, v):
        rows = v.shape[0]
        if heads == 1:
            ref[lead + (slice(row0, row0 + rows), slice(None))] = v
        else:
            dh = ref.shape[-1]
            for h in range(heads):
                ref[lead + (pl.ds(row0 * heads + h, rows, stride=heads), slice(None))] = v[:, h * dh:(h + 1) * dh]

    if prev_ref is None:
        assert ref.shape[-2] == val.shape[0] * heads
        put(0, val)
        return
    r0 = prev_ref.shape[0]
    put(0, prev_ref[...])
    if val is not None:
        assert ref.shape[-2] == val.shape[0] * heads
        put(r0, val[:val.shape[0] - r0, :])
        prev_ref[...] = val[val.shape[0] - r0:, :]


def _rope(x, ct, sa, sb):
    return x * ct + pltpu.roll(x, LANES - MLA_ROPE // 2, 1) * sa + pltpu.roll(x, MLA_ROPE // 2, 1) * sb


def _proj_kernel(*refs, nt, with_head_rows, **kw):
    n_in = 13 + (len(CACHE_KEYS) if with_head_rows else 0)
    cache_refs = refs[n_in:n_in + len(CACHE_KEYS)]
    carry_ref = refs[n_in + 14]
    prev = dict(zip(CACHE_KEYS, refs[n_in + 15:])) if with_head_rows else dict.fromkeys(CACHE_KEYS)
    t = pl.program_id(1)

    @pl.when(t == 0)
    def _():
        carry_ref[...] = refs[4][0]
        if with_head_rows:
            for k, head_ref in zip(CACHE_KEYS, refs[13:n_in]):
                prev[k][...] = head_ref[...]

    if not with_head_rows:
        _proj_tile(refs[:13], refs[n_in:n_in + 15], prev, **kw)
        return

    @pl.when(t < nt)
    def _():
        _proj_tile(refs[:13], refs[n_in:n_in + 15], prev, **kw)

    @pl.when(t == nt)
    def _():
        for k, ref in zip(CACHE_KEYS, cache_refs):
            _store_cache(ref, None, prev[k])


def _proj_tile(in_refs, out_refs, prev, *, cs, q_lora, kv_lora, d_model):
    (x_ref, ct_ref, sa_ref, sb_ref, finit_ref, gmix_ref, w_ref, bexp_ref, gq_ref, wuq_ref,
     gkv_ref, wukv_ref, tri_ref) = in_refs
    (fk_ref, fv_ref, logf_ref, ckv_ref, kr_ref, qf_ref, kfa_ref, vf_ref, qn_ref, qr_ref,
     kc_ref, vm_ref, gate_ref, ftot_ref, carry_ref) = out_refs
    c_ckv = C_CQ + q_lora
    c_kr = c_ckv + kv_lora
    c_gate = c_kr + LANES
    xb = _rms(x_ref[0], gmix_ref[...]).astype(BF16)

    def proj(a, b):
        return jnp.dot(xb, w_ref[:, a:b], preferred_element_type=F32)

    ct, sa, sb = ct_ref[...], sa_ref[...], sb_ref[...]
    nope_w = MLA_HEADS * MLA_NOPE
    cq = proj(C_CQ, c_ckv)
    ckv = proj(c_ckv, c_kr)
    ff = proj(C_FF, C_CQ)
    kr = proj(c_kr, c_gate)

    qf_ref[0] = (proj(C_FQ, C_FK) * FOX_QSCALE).T.astype(BF16)
    fk = proj(C_FK, C_FV)
    _store_cache(fk_ref, fk, prev['fk'])
    fv = proj(C_FV, C_FF)
    _store_cache(fv_ref, fv, prev['fv'])
    vf_ref[0] = fv.T.astype(BF16)

    cqn = _rms(cq, gq_ref[...]).astype(BF16)
    ckvn = _rms(ckv, gkv_ref[...])
    _store_cache(ckv_ref, ckvn, prev['ckv'])
    q = jnp.dot(cqn, wuq_ref[...], preferred_element_type=F32) * MLA_QSCALE
    kv = jnp.dot(ckvn.astype(BF16), wukv_ref[...], preferred_element_type=F32)

    lf = jax.nn.log_sigmoid(ff + bexp_ref[...])
    _store_cache(logf_ref, lf[:, :FOX_HEADS], prev['logf'])
    f, carry = _cumsum_rows(lf, tri_ref[...], carry_ref[0:1, :], cs)
    carry_ref[0:1, :] = carry
    ftot_ref[0] = jnp.broadcast_to(carry, ftot_ref.shape[1:])

    gw = 2 * d_model // GATE_CHUNKS
    for c in range(GATE_CHUNKS):
        gate_ref[0, :, c * gw:(c + 1) * gw] = jax.nn.sigmoid(proj(c_gate + c * gw, c_gate + (c + 1) * gw))

    _store_pairs(kfa_ref, fk.astype(BF16), _forget_columns(f * LOG2E))
    qn_ref[0] = q[:, :nope_w].T.astype(BF16)
    for c in range(MLA_HEADS * MLA_ROPE // LANES):
        qr_ref[0, LANES * c:LANES * (c + 1), :] = _rope(
            q[:, nope_w + LANES * c:nope_w + LANES * (c + 1)], ct, sa, sb).T.astype(BF16)
    vm_ref[0] = kv[:, nope_w:].T.astype(BF16)
    kr4 = _rope(kr, ct, sa, sb)
    _store_cache(kr_ref, kr4[:, :MLA_ROPE], prev['kr'])
    _store_pairs(kc_ref, kv[:, :nope_w].astype(BF16), kr4.astype(BF16))


TRANSPOSED = ('qf', 'vf', 'qn', 'qr', 'vm')
CACHE_KEYS = ('fk', 'fv', 'logf', 'ckv', 'kr')


def _const_spec(shape):
    return pl.BlockSpec(shape, lambda *_: (0,) * len(shape), pipeline_mode=pl.Buffered(1))


def _vmem_limit(block_bytes, const_bytes, temp_bytes):
    need = 2 * block_bytes + const_bytes + temp_bytes
    assert need < VMEM_BYTES_V7X, need
    return int(need)


def _project(x, tables, finit, wts, tri, *, ts, cs, head_rows=None):
    B, n, d = x.shape
    assert n % ts == 0 and ts % cs == 0
    nt = n // ts
    q_lora = wts['g_q'].shape[1]
    kv_lora = wts['g_kv'].shape[1]
    ct, sa, sb = tables
    tile = (lambda t: jnp.minimum(t, nt - 1)) if head_rows is not None else (lambda t: t)
    row = lambda w: pl.BlockSpec((1, ts, w), lambda b, t: (b, tile(t), 0))
    tab = pl.BlockSpec((ts, LANES), lambda b, t: (tile(t), 0))
    in_specs = [row(d), tab, tab, tab,
                pl.BlockSpec((1, 8, LANES), lambda b, t: (0, 0, 0)),
                _const_spec(wts['g_mix'].shape), _const_spec(wts['w_a'].shape), _const_spec(wts['b_exp'].shape),
                _const_spec(wts['g_q'].shape), _const_spec(wts['w_uq'].shape), _const_spec(wts['g_kv'].shape),
                _const_spec(wts['w_ukv'].shape), _const_spec(tri.shape)]
    widths = dict(fk=(FOX_W, F32), fv=(FOX_W, F32), logf=(FOX_HEADS, F32), ckv=(kv_lora, F32), kr=(MLA_ROPE, F32),
                  qf=(FOX_W, BF16), kfa=(N_PAIRS * PAIR_W, BF16), vf=(FOX_W, BF16),
                  qn=(MLA_HEADS * MLA_NOPE, BF16), qr=(MLA_HEADS * MLA_ROPE, BF16),
                  kc=(N_PAIRS * PAIR_W, BF16), vm=(MLA_HEADS * MLA_V, BF16), gate=(2 * d, F32))
    col = lambda w: pl.BlockSpec((1, w, ts), lambda b, t: (b, 0, tile(t)))
    out_shape = [jax.ShapeDtypeStruct((B, w, n) if k in TRANSPOSED else (B, n, w), dt) for k, (w, dt) in widths.items()]
    out_specs = [col(w) if k in TRANSPOSED else row(w) for k, (w, _) in widths.items()]
    args = [x, ct, sa, sb, finit, wts['g_mix'], wts['w_a'], wts['b_exp'], wts['g_q'], wts['w_uq'], wts['g_kv'],
            wts['w_ukv'], tri]
    scratch = [pltpu.VMEM((8, LANES), F32)]
    if head_rows is not None:
        r0 = head_rows[CACHE_KEYS[0]].shape[0]
        assert r0 % 8 == 0 and ts % 8 == 0 and r0 < ts
        for idx, k in enumerate(CACHE_KEYS):
            heads, w = (FOX_HEADS, FOX_HEAD_DIM) if k in ('fk', 'fv') else (1, widths[k][0])
            out_shape[idx] = jax.ShapeDtypeStruct((1, B, (r0 + n) * heads, w), F32)
            out_specs[idx] = pl.BlockSpec((1, 1, ts * heads, w), lambda b, t: (0, b, t, 0))
            args.append(head_rows[k])
            in_specs.append(_const_spec(head_rows[k].shape))
            scratch.append(pltpu.VMEM(head_rows[k].shape, F32))
    out_shape.append(jax.ShapeDtypeStruct((B, 8, LANES), F32))
    out_specs.append(pl.BlockSpec((1, 8, LANES), lambda b, t: (b, 0, 0)))
    block_bytes = ts * (d * 4 + 3 * LANES * 4 + sum(w * jnp.dtype(dt).itemsize for w, dt in widths.values()))
    if head_rows is not None:
        block_bytes += 2 * ts * FOX_HEADS * LANES * 4
    const_bytes = sum(int(np.prod(wts[k].shape)) * wts[k].dtype.itemsize for k in ('w_a', 'w_uq', 'w_ukv')) + tri.size * 2
    temp_bytes = ts * (2 * d + 2 * d) * 4 * 2
    outs = pl.pallas_call(
        functools.partial(_proj_kernel, cs=cs, q_lora=q_lora, kv_lora=kv_lora, d_model=d, nt=nt,
                          with_head_rows=head_rows is not None),
        grid=(B, nt + (head_rows is not None)), in_specs=in_specs, out_specs=out_specs, out_shape=out_shape,
        scratch_shapes=scratch,
        compiler_params=pltpu.CompilerParams(
            dimension_semantics=("arbitrary", "arbitrary"),
            vmem_limit_bytes=_vmem_limit(block_bytes, const_bytes, temp_bytes)),
        name="proj",
    )(*args)
    res = dict(zip(widths.keys(), outs[:-1]))
    res['ftot'] = outs[-1]
    return res


def _prefix_kernel(ckt_ref, cvt_ref, lft_ref, ckv_ref, krt_ref, wukv_ref, tri_ref,
                   kfa_ref, vf_ref, kc_ref, vm_ref, *, cs):
    P = ckt_ref.shape[2]
    lft = lft_ref[0]
    lf_exp = jnp.concatenate([lft] * F_PARTS + [jnp.zeros((LANES - F_PARTS * FOX_HEADS, P), F32)], axis=0).T
    f, total = _cumsum_rows(lf_exp, tri_ref[...], jnp.zeros((1, LANES), F32), cs)
    _store_pairs(kfa_ref, ckt_ref[0].T.astype(BF16), _forget_columns((f - total) * LOG2E))
    vf_ref[0] = cvt_ref[0].astype(BF16)
    kv = jnp.dot(ckv_ref[0].astype(BF16), wukv_ref[...], preferred_element_type=F32)
    nope_w = MLA_HEADS * MLA_NOPE
    vm_ref[0] = kv[:, nope_w:].T.astype(BF16)
    kr4 = jnp.concatenate([krt_ref[0]] * (LANES // MLA_ROPE), axis=0).T
    _store_pairs(kc_ref, kv[:, :nope_w].astype(BF16), kr4.astype(BF16))


def _prefix_operands(ckt, cvt, lft, cckv, krt, w_ukv, tri, *, cs):
    B, _, P = ckt.shape
    row = lambda r, w: pl.BlockSpec((1, r, w), lambda b: (b, 0, 0))
    widths = dict(kfa=N_PAIRS * PAIR_W, vf=FOX_W, kc=N_PAIRS * PAIR_W, vm=MLA_HEADS * MLA_V)
    ins = (ckt, cvt, lft, cckv, krt)
    block_bytes = sum(max(a.shape[1], 8) * max(a.shape[2], LANES) * 4 for a in ins) + P * sum(widths.values()) * 2
    outs = pl.pallas_call(
        functools.partial(_prefix_kernel, cs=cs),
        grid=(B,),
        in_specs=[row(a.shape[1], a.shape[2]) for a in ins] + [_const_spec(w_ukv.shape), _const_spec(tri.shape)],
        out_specs=[row(w, P) if k in TRANSPOSED else row(P, w) for k, w in widths.items()],
        out_shape=[jax.ShapeDtypeStruct((B, w, P) if k in TRANSPOSED else (B, P, w), BF16) for k, w in widths.items()],
        compiler_params=pltpu.CompilerParams(
            dimension_semantics=("arbitrary",),
            vmem_limit_bytes=_vmem_limit(block_bytes, w_ukv.size * 2 + tri.size * 2, P * 1024 * 4 * 4)),
        name="prefix",
    )(*ins, w_ukv, tri)
    return dict(zip(widths.keys(), outs))


def _scores(ks, qts):
    return tuple(jnp.dot(k, qt, preferred_element_type=F32) for k, qt in zip(ks, qts))


def _online_update(states, scores, vts, masks=None, extra=None):
    mids = []
    for c, ((m, l, acc), s) in enumerate(zip(states, scores)):
        if masks is not None:
            s = jnp.where(masks[c], s, NEG)
        n_extra = 0
        if extra is not None:
            n_extra = extra[0][c].shape[0]
            s = jnp.concatenate([extra[0][c], s], axis=0)
        m_new = jnp.maximum(m, jnp.max(s, axis=0, keepdims=True))
        alpha = jnp.exp2(m - m_new)
        p = jnp.exp2(s - m_new)
        mids.append((m_new, alpha * l + jnp.sum(p, axis=0, keepdims=True), alpha * acc, p.astype(BF16), n_extra))
    out = []
    for c, ((m_new, l, acc, p, n_extra), vt) in enumerate(zip(mids, vts)):
        if n_extra:
            acc = acc + jnp.dot(extra[1][c], p[:n_extra], preferred_element_type=F32)
        out.append((m_new, l, acc + jnp.dot(vt, p[n_extra:], preferred_element_type=F32)))
    return tuple(out)


def _attn_kernel(*refs, tq, tk, tkp, nq, pps, has_prefix, n_pref, transpose_out):
    if has_prefix:
        (qf_ref, qn_ref, qr_ref, kfa_ref, vf_ref, kc_ref, vm_ref,
         pkfa_ref, pvf_ref, pkc_ref, pvm_ref, of_ref, om_ref, sa_ref, sb_ref) = refs
    else:
        qf_ref, qn_ref, qr_ref, kfa_ref, vf_ref, kc_ref, vm_ref, of_ref, om_ref, sa_ref, sb_ref = refs
        pkfa_ref = pvf_ref = pkc_ref = pvm_ref = None
    b, j, i = pl.program_id(0), pl.program_id(1), pl.program_id(2)
    pref_bias = None if (not has_prefix or n_pref is None) else jnp.where(b < n_pref, 0.0, NEG)

    frow = lax.broadcasted_iota(jnp.int32, (LANES, tq), 0)
    zero_half = jnp.zeros((FOX_HEAD_DIM, tq), BF16)
    rope_groups = LANES // MLA_ROPE
    rope_rows_per_step = pps * HEADS_PER_PAIR * MLA_ROPE
    chains = []
    for mixer in range(2):
        for p in range(pps):
            for s in range(HEADS_PER_PAIR):
                h = HEADS_PER_PAIR * (pps * j + p) + s
                vrows = slice(p * LANES + s * FOX_HEAD_DIM, p * LANES + (s + 1) * FOX_HEAD_DIM)
                own = [zero_half] * HEADS_PER_PAIR
                own[s] = (qf_ref if mixer == 0 else qn_ref)[0, vrows, :]
                if mixer == 0:
                    extra = jnp.where((frow < F_PARTS * FOX_HEADS) & ((frow & (FOX_HEADS - 1)) == h),
                                      -1.0, 0.0).astype(BF16)
                else:
                    r0 = (p * HEADS_PER_PAIR * MLA_ROPE // LANES) * LANES if rope_rows_per_step > LANES else 0
                    qr = qr_ref[0, r0:r0 + LANES, :]
                    extra = jnp.where((frow >> int(math.log2(MLA_ROPE))) == (h & (rope_groups - 1)), qr, jnp.zeros_like(qr))
                chains.append((jnp.concatenate(own + [extra], axis=0), slice(p * PAIR_W, (p + 1) * PAIR_W), vrows, mixer))
    qts = [c[0] for c in chains]
    key_refs, val_refs = (kfa_ref, kc_ref), (vf_ref, vm_ref)
    pkey_refs, pval_refs = (pkfa_ref, pkc_ref), (pvf_ref, pvm_ref)

    def scores(krefs, rows):
        return _scores([krefs[c[3]][0, rows, c[1]] for c in chains], qts)

    def update(sts, sc, vrefs, rows, **kw):
        return _online_update(sts, sc, [vrefs[c[3]][0, c[2], rows] for c in chains], **kw)

    assert tq == tk
    tile_rows = (lambda t: pl.ds(pl.multiple_of(t * tk, tk), tk)) if nq > 1 else (lambda t: slice(0, tk))

    def put_scores(ref, t):
        for c, s in enumerate(scores(key_refs, tile_rows(t))):
            ref[c] = s

    def step(sts, ref, t, **kw):
        return update(sts, [ref[c] for c in range(len(chains))], val_refs, tile_rows(t), **kw)

    prefix = None
    if has_prefix:
        assert tkp == pkfa_ref.shape[1]
        all_rows = slice(0, tkp)
        psc = scores(pkey_refs, all_rows)
        if pref_bias is not None:
            psc = [s + pref_bias for s in psc]
        prefix = (psc, [pval_refs[c[3]][0, c[2], all_rows] for c in chains])
    put_scores(sa_ref, 0)

    def finish(sts, ref):
        krow = lax.broadcasted_iota(jnp.int32, (tq, tq), 0)
        qcol = lax.broadcasted_iota(jnp.int32, (tq, tq), 1)
        chunk_shift = int(math.log2(CHUNK))
        masks = (krow <= qcol, (krow >> chunk_shift) <= (qcol >> chunk_shift))
        sts = step(sts, ref, i, masks=[masks[c[3]] for c in chains], extra=prefix)
        outs = [acc / l for _, l, acc in sts]
        half = len(outs) // 2
        for o_ref, group in ((of_ref, outs[:half]), (om_ref, outs[half:])):
            ot = jnp.concatenate(group, axis=0)
            o_ref[0] = (ot.T if transpose_out else ot).astype(BF16)

    init = (jnp.full((1, tq), NEG, F32), jnp.zeros((1, tq), F32), jnp.zeros((FOX_HEAD_DIM, tq), F32))
    states = tuple(init for _ in chains)
    if nq == 1:
        finish(states, sa_ref)
    else:
        def body(u, sts):
            put_scores(sb_ref, 2 * u + 1)
            sts = step(sts, sa_ref, 2 * u)
            put_scores(sa_ref, 2 * u + 2)
            return step(sts, sb_ref, 2 * u + 1)

        states = lax.fori_loop(0, i >> 1, body, states)

        @pl.when((i & 1) == 0)
        def _():
            finish(states, sa_ref)

        @pl.when((i & 1) == 1)
        def _():
            put_scores(sb_ref, i)
            finish(step(states, sa_ref, i - 1), sb_ref)


def _attention(ops, prefix, *, tq, tk, tkp, pps, n_pref=None):
    B, n, _ = ops['kfa'].shape
    assert n % tq == 0 and tq % tk == 0 and N_PAIRS % pps == 0
    nq = n // tq
    transpose_out = tq % LANES == 0
    rope_w = max(LANES, pps * HEADS_PER_PAIR * MLA_ROPE)
    steps_per_rope_block = rope_w // (pps * HEADS_PER_PAIR * MLA_ROPE)
    qspec = pl.BlockSpec((1, pps * LANES, tq), lambda b, j, i: (b, j, i))
    qrspec = pl.BlockSpec((1, rope_w, tq), lambda b, j, i: (b, j // steps_per_rope_block, i))
    kspec = pl.BlockSpec((1, n, pps * PAIR_W), lambda b, j, i: (b, 0, j))
    vspec = pl.BlockSpec((1, pps * LANES, n), lambda b, j, i: (b, j, 0))
    args = [ops['qf'], ops['qn'], ops['qr'], ops['kfa'], ops['vf'], ops['kc'], ops['vm']]
    in_specs = [qspec, qspec, qrspec, kspec, vspec, kspec, vspec]
    block_bytes = 2 * pps * (3 * tq * LANES + 2 * n * (PAIR_W + LANES) + 2 * tq * LANES)
    if prefix is not None:
        Bp, P, _ = prefix['kfa'].shape
        assert P % tkp == 0
        if Bp == 1:
            pb = lambda b: 0
        elif n_pref is not None:
            pb = lambda b: jnp.minimum(b, n_pref - 1)
        else:
            pb = lambda b: b
        pk = pl.BlockSpec((1, P, pps * PAIR_W), lambda b, j, i: (pb(b), 0, j))
        pv = pl.BlockSpec((1, pps * LANES, P), lambda b, j, i: (pb(b), j, 0))
        args += [prefix['kfa'], prefix['vf'], prefix['kc'], prefix['vm']]
        in_specs += [pk, pv, pk, pv]
        block_bytes += 2 * 2 * pps * P * (PAIR_W + LANES)
    if transpose_out:
        ospec = pl.BlockSpec((1, tq, pps * LANES), lambda b, j, i: (b, i, j))
        oshape = (B, n, N_PAIRS * LANES)
    else:
        ospec = pl.BlockSpec((1, pps * LANES, tq), lambda b, j, i: (b, j, i))
        oshape = (B, N_PAIRS * LANES, n)
    n_chains = 2 * HEADS_PER_PAIR * pps
    tq_pad = max(tq, LANES)
    temp_bytes = n_chains * 6 * tq_pad * max(tk, tkp if prefix is not None else tk) * 4 + 4 * n_chains * tq_pad * LANES * 4
    of, om = pl.pallas_call(
        functools.partial(_attn_kernel, tq=tq, tk=tk, tkp=tkp, has_prefix=prefix is not None, n_pref=n_pref,
                          nq=nq, pps=pps, transpose_out=transpose_out),
        grid=(B, N_PAIRS // pps, nq), in_specs=in_specs, out_specs=[ospec, ospec],
        out_shape=[jax.ShapeDtypeStruct(oshape, BF16), jax.ShapeDtypeStruct(oshape, BF16)],
        scratch_shapes=[pltpu.VMEM((n_chains, tk, tq), F32), pltpu.VMEM((n_chains, tk, tq), F32)],
        compiler_params=pltpu.CompilerParams(
            dimension_semantics=("arbitrary", "arbitrary", "arbitrary"),
            vmem_limit_bytes=_vmem_limit(block_bytes, 2 * n_chains * tk * tq * 4, temp_bytes)),
        name="attn",
    )(*args)
    if not transpose_out:
        of, om = jnp.swapaxes(of, 1, 2), jnp.swapaxes(om, 1, 2)
    return of, om


def _ffn_kernel(*refs, d_ff, fc, flat_group):
    if flat_group:
        (h_ref, of_ref, om_ref, g_ref, left_ref, wof_ref, wom_ref, wout_ref, gffn_ref, wup_ref,
         cw_ref, cb_ref, wdown_ref, gfin_ref, y_ref, u_ref, l1_ref, l2_ref, _) = refs
        l1_ref[...] = jnp.zeros_like(l1_ref)
        l2_ref[...] = jnp.zeros_like(l2_ref)
        for g in range(left_ref.shape[0]):
            r0 = g * flat_group
            l1_ref[r0:r0 + 1, :] = left_ref[g, 1:2, :]
            l2_ref[r0:r0 + 1, :] = left_ref[g, 0:1, :]
            l2_ref[r0 + 1:r0 + 2, :] = left_ref[g, 1:2, :]
        ub_ref = refs[-1]
        ub_ref[0:HALO, :] = jnp.zeros((HALO, ub_ref.shape[1]), F32)
    else:
        (h_ref, of_ref, om_ref, g_ref, left_ref, wof_ref, wom_ref, wout_ref, gffn_ref, wup_ref,
         cw_ref, cb_ref, wdown_ref, gfin_ref, y_ref, u_ref, ub_ref) = refs

        @pl.when(pl.program_id(1) == 0)
        def _():
            ub_ref[HALO - (CONV_W - 1):HALO, :] = left_ref[0]

    ts, d = h_ref.shape[1], h_ref.shape[2]
    ya = jnp.dot(of_ref[0], wof_ref[...], preferred_element_type=F32)
    yb = jnp.dot(om_ref[0], wom_ref[...], preferred_element_type=F32)
    mix = g_ref[0, :, :d] * ya + g_ref[0, :, d:] * yb
    h1 = h_ref[0] + jnp.dot(mix.astype(BF16), wout_ref[...], preferred_element_type=F32)
    xn = _rms(h1, gffn_ref[...]).astype(BF16)

    if flat_group:
        row = lax.broadcasted_iota(jnp.int32, (ts, 1), 0)
        keep1 = ((row & (flat_group - 1)) >= 1).astype(F32)
        keep2 = ((row & (flat_group - 1)) >= 2).astype(F32)

    def up(c0):
        return jnp.dot(xn, wup_ref[:, c0:c0 + fc], preferred_element_type=F32)

    def conv(u, c0):
        cols = slice(c0, c0 + fc)
        ub_ref[HALO:HALO + ts, cols] = u
        u1 = ub_ref[HALO - 1:HALO - 1 + ts, cols]
        u2 = ub_ref[HALO - 2:HALO - 2 + ts, cols]
        if flat_group:
            u_ref[0, :, cols] = u
            u1 = u1 * keep1 + l1_ref[:, cols]
            u2 = u2 * keep2 + l2_ref[:, cols]
        return cb_ref[:, cols] + u2 * cw_ref[0:1, cols] + u1 * cw_ref[1:2, cols] + u * cw_ref[2:3, cols]

    n_chunks = d_ff // fc
    acc = jnp.zeros((ts, d), F32)
    ups = [(up(c * fc), up(d_ff + c * fc)) for c in range(min(UP_AHEAD, n_chunks))]
    for c in range(n_chunks):
        u_gate, u_val = ups.pop(0)
        if c + UP_AHEAD < n_chunks:
            ups.append((up((c + UP_AHEAD) * fc), up(d_ff + (c + UP_AHEAD) * fc)))
        act = (jax.nn.silu(conv(u_gate, c * fc)) * conv(u_val, d_ff + c * fc)).astype(BF16)
        acc = acc + jnp.dot(act, wdown_ref[c * fc:(c + 1) * fc, :], preferred_element_type=F32)
    y_ref[0] = _rms(h1 + acc, gfin_ref[...])
    if not flat_group:
        tail = ub_ref[HALO + ts - (CONV_W - 1):HALO + ts, :]
        u_ref[0] = tail
        ub_ref[HALO - (CONV_W - 1):HALO, :] = tail


def _merge_ffn(h, of, om, gates, left, wts, *, ts, flat_group=0):
    B, n, d = h.shape
    d_ff = wts['w_down'].shape[0]
    fc = MXU_DIM
    assert n % ts == 0 and d_ff % fc == 0
    row = lambda w: pl.BlockSpec((1, ts, w), lambda b, t: (b, t, 0))
    consts = [wts[k] for k in ('w_o_fox', 'w_o_mla', 'w_out', 'g_ffn', 'w_up', 'conv_w', 'conv_b', 'w_down', 'g_fin')]
    const_specs = [_const_spec(c.shape) for c in consts]
    in_specs = [row(d), row(of.shape[2]), row(om.shape[2]), row(2 * d)]
    args = [h, of, om, gates, left]
    if flat_group:
        assert B == 1 and ts % flat_group == 0 and left.shape[0] * flat_group == n
        gpt = ts // flat_group
        in_specs.append(pl.BlockSpec((gpt, CONV_W - 1, 2 * d_ff), lambda b, t: (t, 0, 0)))
        u_shape, u_spec = (1, n, 2 * d_ff), row(2 * d_ff)
        scratch = [pltpu.VMEM((ts, 2 * d_ff), F32), pltpu.VMEM((ts, 2 * d_ff), F32)]
        const_extra = 2 * ts * 2 * d_ff * 4 + 2 * gpt * 8 * 2 * d_ff * 4
    else:
        bl = left.shape[0]
        in_specs.append(pl.BlockSpec((1, CONV_W - 1, 2 * d_ff), lambda b, t: (b if bl > 1 else 0, 0, 0)))
        u_shape = (B, CONV_W - 1, 2 * d_ff)
        u_spec = pl.BlockSpec((1, CONV_W - 1, 2 * d_ff), lambda b, t: (b, 0, 0))
        scratch = []
        const_extra = 0
    scratch.append(pltpu.VMEM((HALO + ts, 2 * d_ff), F32))
    const_extra += (HALO + ts) * 2 * d_ff * 4
    u_rows = ts if flat_group else 8
    block_bytes = ts * (d * 4 + (of.shape[2] + om.shape[2]) * 2 + 2 * d * 4 + d * 4) + u_rows * 2 * d_ff * 4
    const_bytes = sum(c.size * c.dtype.itemsize for c in consts) + const_extra
    temp_bytes = ts * d * 4 * 8 + ts * fc * 4 * 16
    y, u = pl.pallas_call(
        functools.partial(_ffn_kernel, d_ff=d_ff, fc=fc, flat_group=flat_group),
        grid=(B, n // ts), in_specs=in_specs + const_specs,
        out_specs=[row(d), u_spec],
        out_shape=[jax.ShapeDtypeStruct((B, n, d), F32), jax.ShapeDtypeStruct(u_shape, F32)],
        scratch_shapes=scratch,
        compiler_params=pltpu.CompilerParams(
            dimension_semantics=("arbitrary", "arbitrary"),
            vmem_limit_bytes=_vmem_limit(block_bytes, const_bytes, temp_bytes)),
        name="ffn",
    )(*args, *consts)
    return y, u


def _rope_tables(pos):
    half = MLA_ROPE // 2
    inv = ROPE_BASE ** (-jnp.arange(0, MLA_ROPE, 2, dtype=F32) / MLA_ROPE)
    ang = pos.astype(F32)[:, None] * inv[None, :]
    cos, sin = jnp.cos(ang), jnp.sin(ang)
    zero = jnp.zeros_like(sin)
    reps = LANES // MLA_ROPE
    ct = jnp.tile(jnp.concatenate([cos, cos], axis=1), (1, reps))
    sa = jnp.tile(jnp.concatenate([-sin, zero], axis=1), (1, reps))
    sb = jnp.tile(jnp.concatenate([zero, sin], axis=1), (1, reps))
    assert half * 2 == MLA_ROPE
    return ct, sa, sb


def _lower_tri(n, group):
    i = np.arange(n)
    m = (i[None, :] <= i[:, None]) & ((i[None, :] // group) == (i[:, None] // group))
    return jnp.asarray(m, BF16)


def _prepare_weights(norm_mix_g, w_in, b_forget, mla_q_norm_g, w_uq, mla_kv_norm_g, w_ukv, w_o_fox, w_o_mla, w_out,
                     norm_ffn_g, w_up, conv_w, conv_b, w_down, norm_final_g):
    d = w_in.shape[0]
    q_lora, kv_lora = mla_q_norm_g.shape[0], mla_kv_norm_g.shape[0]
    off_ff = 3 * FOX_W
    off_cq = off_ff + FOX_HEADS
    off_ckv = off_cq + q_lora
    off_kr = off_ckv + kv_lora
    off_gate = off_kr + MLA_ROPE
    pad = LANES - F_PARTS * FOX_HEADS
    w_ff = jnp.pad(jnp.tile(w_in[:, off_ff:off_cq], (1, F_PARTS)), ((0, 0), (0, pad)))
    w_a = jnp.concatenate([
        w_in[:, :off_ff], w_ff, w_in[:, off_cq:off_kr],
        jnp.tile(w_in[:, off_kr:off_gate], (1, LANES // MLA_ROPE)),
        w_in[:, off_gate:]], axis=1).astype(BF16)
    uq = w_uq.reshape(q_lora, MLA_HEADS, MLA_QK)
    w_uq_p = jnp.concatenate([uq[:, :, :MLA_NOPE].reshape(q_lora, -1), uq[:, :, MLA_NOPE:].reshape(q_lora, -1)], axis=1)
    ukv = w_ukv.reshape(kv_lora, MLA_HEADS, MLA_NOPE + MLA_V)
    w_ukv_p = jnp.concatenate([ukv[:, :, :MLA_NOPE].reshape(kv_lora, -1), ukv[:, :, MLA_NOPE:].reshape(kv_lora, -1)], axis=1)
    return dict(
        g_mix=norm_mix_g.reshape(1, d), w_a=w_a,
        b_exp=jnp.pad(jnp.tile(b_forget, F_PARTS), (0, pad)).reshape(1, LANES),
        g_q=mla_q_norm_g.reshape(1, q_lora), w_uq=w_uq_p.astype(BF16),
        g_kv=mla_kv_norm_g.reshape(1, kv_lora), w_ukv=w_ukv_p.astype(BF16),
        w_o_fox=w_o_fox.astype(BF16), w_o_mla=w_o_mla.astype(BF16), w_out=w_out.astype(BF16),
        g_ffn=norm_ffn_g.reshape(1, d), w_up=w_up.astype(BF16), conv_w=conv_w, conv_b=conv_b.reshape(1, -1),
        w_down=w_down.astype(BF16), g_fin=norm_final_g.reshape(1, d))


PROJ_ROWS = 512
FFN_ROWS = 256
PROMPT_TQ = 256
PROMPT_PAIRS = 2
SMALL_ROWS = LANES
SMALL_PAIRS = 2


def kernel(x_prompt, x_sample, cache_fox_k, cache_fox_v, cache_fox_logf, cache_mla_ckv, cache_mla_krope, state_ffn_conv, meta_tokens, norm_mix_g, w_in, b_forget, mla_q_norm_g, w_uq, mla_kv_norm_g, w_ukv, w_o_fox, w_o_mla, w_out, norm_ffn_g, w_up, conv_w, conv_b, w_down, norm_final_g):
    assert w_in.shape[0] == 1, "single-layer model"
    B, S, d = x_prompt.shape
    Bs, ns, _ = x_sample.shape
    P = cache_fox_k.shape[2]
    n_meta = meta_tokens.shape[0]
    assert n_meta == N_META == ns
    wts = _prepare_weights(norm_mix_g[0], w_in[0], b_forget[0], mla_q_norm_g[0], w_uq[0], mla_kv_norm_g[0], w_ukv[0],
                           w_o_fox[0], w_o_mla[0], w_out[0], norm_ffn_g[0], w_up[0], conv_w[0], conv_b[0], w_down[0],
                           norm_final_g)
    d_ff2 = w_up.shape[2]

    r_pad = -(-(Bs + 1) * ns // SMALL_ROWS) * SMALL_ROWS
    n_groups = r_pad // ns
    n_fill = r_pad - (Bs + 1) * ns
    rows = jnp.concatenate([x_sample.reshape(Bs * ns, d), jnp.zeros((n_fill, d), x_sample.dtype),
                            meta_tokens.astype(x_sample.dtype)], axis=0)[None]
    pos_small = jnp.concatenate([jnp.tile(P + jnp.arange(ns), Bs), jnp.zeros((n_fill,), jnp.int32), jnp.arange(n_meta)])
    zero_f = jnp.zeros((1, 8, LANES), F32)
    sm = _project(rows, _rope_tables(pos_small), zero_f, wts, _lower_tri(r_pad, ns), ts=r_pad, cs=r_pad)
    sm_b = {k: (jnp.swapaxes(v.reshape(v.shape[1], n_groups, ns), 0, 1) if k in TRANSPOSED
                else v.reshape(n_groups, ns, v.shape[-1])) for k, v in sm.items() if k != 'ftot'}
    live = lambda a: jnp.concatenate([a[:Bs], a[n_groups - 1:]], axis=0)

    feat_major = lambda a: jnp.moveaxis(a[0], 1, -1).reshape(Bs, -1, P)
    prefix_s = _prefix_operands(
        feat_major(cache_fox_k), feat_major(cache_fox_v), feat_major(cache_fox_logf), cache_mla_ckv[0],
        feat_major(cache_mla_krope), wts['w_ukv'], _lower_tri(MXU_DIM, MXU_DIM), cs=MXU_DIM)
    of_s, om_s = _attention({k: live(v) for k, v in sm_b.items()}, prefix_s, tq=ns, tk=ns, tkp=P, pps=SMALL_PAIRS, n_pref=Bs)

    def spread(a):
        return jnp.concatenate([a[:Bs].reshape(Bs * ns, -1), jnp.zeros((n_fill, a.shape[-1]), a.dtype), a[Bs]], axis=0)[None]

    left_groups = jnp.concatenate([state_ffn_conv[0], jnp.zeros((n_groups - Bs, CONV_W - 1, d_ff2), F32)], axis=0)
    y_s, u_s = _merge_ffn(rows, spread(of_s), spread(om_s), sm['gate'], left_groups, wts, ts=SMALL_ROWS, flat_group=ns)
    y_sample = y_s[0, :Bs * ns].reshape(Bs, ns, d)
    u_groups = u_s.reshape(n_groups, ns, d_ff2)
    conv_s = u_groups[:Bs, ns - (CONV_W - 1):]
    left_meta = u_groups[n_groups - 1:, ns - (CONV_W - 1):]

    meta_ops = {k: sm_b[k][n_groups - 1:] for k in ('kfa', 'vf', 'kc', 'vm')}
    pos_p = n_meta + jnp.arange(S)
    pr = _project(x_prompt, _rope_tables(pos_p), sm['ftot'], wts, _lower_tri(MXU_DIM, MXU_DIM), ts=PROJ_ROWS,
                  cs=MXU_DIM, head_rows={k: sm_b[k][n_groups - 1] for k in CACHE_KEYS})
    of_p, om_p = _attention(pr, meta_ops, tq=PROMPT_TQ, tk=PROMPT_TQ, tkp=n_meta, pps=PROMPT_PAIRS)
    y_prompt, conv_p = _merge_ffn(x_prompt, of_p, om_p, pr['gate'], left_meta, wts, ts=FFN_ROWS)
    caches_p = [pr[k].reshape((1, B, n_meta + S) + ((FOX_HEADS, FOX_HEAD_DIM) if k in ('fk', 'fv') else pr[k].shape[3:]))
                for k in CACHE_KEYS]

    def sample_rows(name, tail):
        return sm_b[name][:Bs].reshape((1, Bs, ns) + tail)

    hd = (FOX_HEADS, FOX_HEAD_DIM)
    return (y_prompt, y_sample, *caches_p, conv_p[None],
            sample_rows('fk', hd), sample_rows('fv', hd), sample_rows('logf', (FOX_HEADS,)),
            sample_rows('ckv', (cache_mla_ckv.shape[-1],)), sample_rows('kr', (MLA_ROPE,)), conv_s[None])
```

```python
import functools
import math

import jax
import jax.numpy as jnp
import numpy as np
from jax import lax
from jax.experimental import pallas as pl
from jax.experimental.pallas import tpu as pltpu

F32 = jnp.float32
BF16 = jnp.bfloat16

N_META = 16
CHUNK = 64
EPS = 1e-6
NEG = -1e30
FOX_HEADS = 8
FOX_HEAD_DIM = 64
MLA_HEADS = 8
MLA_NOPE = 64
MLA_ROPE = 32
MLA_V = 64
MLA_QK = MLA_NOPE + MLA_ROPE
ROPE_BASE = 10000.0
CONV_W = 3

FOX_W = FOX_HEADS * FOX_HEAD_DIM
LOG2E = math.log2(math.e)
FOX_QSCALE = LOG2E / math.sqrt(FOX_HEAD_DIM)
MLA_QSCALE = LOG2E / math.sqrt(MLA_QK)

LANES = 128
MXU_DIM = 256
VMEM_BYTES_V7X = 64 * 1024 * 1024
F_PARTS = 3
HEADS_PER_PAIR = LANES // FOX_HEAD_DIM
N_PAIRS = FOX_HEADS // HEADS_PER_PAIR
PAIR_W = 2 * LANES
HALO = 8
GATE_CHUNKS = 4
UP_AHEAD = 3

C_FQ = 0
C_FK = C_FQ + FOX_W
C_FV = C_FK + FOX_W
C_FF = C_FV + FOX_W
C_CQ = C_FF + LANES


def _rms(x, g):
    return x * lax.rsqrt(jnp.mean(x * x, axis=-1, keepdims=True) + EPS) * g


def _split3(x):
    hi = x.astype(BF16)
    r = x - hi.astype(F32)
    mid = r.astype(BF16)
    lo = (r - mid.astype(F32)).astype(BF16)
    return hi, mid, lo


def _cumsum_rows(lf, tri, carry, cs):
    hi, mid, lo = _split3(lf)
    pieces = []
    for c in range(lf.shape[0] // cs):
        sl = slice(c * cs, (c + 1) * cs)
        fc = (jnp.dot(tri, hi[sl], preferred_element_type=F32)
              + jnp.dot(tri, mid[sl], preferred_element_type=F32)
              + jnp.dot(tri, lo[sl], preferred_element_type=F32)) + carry
        carry = fc[cs - 1:cs, :]
        pieces.append(fc)
    f = pieces[0] if len(pieces) == 1 else jnp.concatenate(pieces, axis=0)
    return f, carry


def _forget_columns(f):
    hi, mid, lo = _split3(f)
    lane = lax.broadcasted_iota(jnp.int32, f.shape, 1)
    zero = jnp.zeros_like(hi)
    return jnp.where(lane < FOX_HEADS, hi,
                     jnp.where(lane < 2 * FOX_HEADS, mid,
                               jnp.where(lane < F_PARTS * FOX_HEADS, lo, zero)))


def _store_pairs(ref, main, extra):
    for j in range(N_PAIRS):
        ref[0, :, PAIR_W * j:PAIR_W * j + LANES] = main[:, LANES * j:LANES * (j + 1)]
        ref[0, :, PAIR_W * j + LANES:PAIR_W * (j + 1)] = extra


def _store_cache(ref, val, prev_ref=None):
    lead = (0,) * (len(ref.shape) - 2)
    width = prev_ref.shape[1] if val is None else val.shape[1]
    heads = width // ref.shape[-1]

    def put(row0, v):
        rows = v.shape[0]
        if heads == 1:
            ref[lead + (slice(row0, row0 + rows), slice(None))] = v
        else:
            dh = ref.shape[-1]
            for h in range(heads):
                ref[lead + (pl.ds(row0 * heads + h, rows, stride=heads), slice(None))] = v[:, h * dh:(h + 1) * dh]

    if prev_ref is None:
        assert ref.shape[-2] == val.shape[0] * heads
        put(0, val)
        return
    r0 = prev_ref.shape[0]
    put(0, prev_ref[...])
    if val is not None:
        assert ref.shape[-2] == val.shape[0] * heads
        put(r0, val[:val.shape[0] - r0, :])
        prev_ref[...] = val[val.shape[0] - r0:, :]


def _rope(x, ct, sa, sb):
    return x * ct + pltpu.roll(x, LANES - MLA_ROPE // 2, 1) * sa + pltpu.roll(x, MLA_ROPE // 2, 1) * sb


def _proj_kernel(*refs, nt, with_head_rows, **kw):
    n_in = 13 + (len(CACHE_KEYS) if with_head_rows else 0)
    cache_refs = refs[n_in:n_in + len(CACHE_KEYS)]
    carry_ref = refs[n_in + 14]
    prev = dict(zip(CACHE_KEYS, refs[n_in + 15:])) if with_head_rows else dict.fromkeys(CACHE_KEYS)
    t = pl.program_id(1)

    @pl.when(t == 0)
    def _():
        carry_ref[...] = refs[4][0]
        if with_head_rows:
            for k, head_ref in zip(CACHE_KEYS, refs[13:n_in]):
                prev[k][...] = head_ref[...]

    tile_refs = (refs[:13], refs[n_in:n_in + 15], prev)
    if not with_head_rows:
        _proj_tile(*tile_refs, **kw)
        return

    @pl.when(t < nt - 1)
    def _():
        _proj_tile(*tile_refs, **kw)

    @pl.when(t == nt - 1)
    def _():
        _proj_tile(*tile_refs, gates=False, **kw)

    @pl.when(t == nt)
    def _():
        _proj_tile(*tile_refs, main=False, **kw)
        for k, ref in zip(CACHE_KEYS, cache_refs):
            _store_cache(ref, None, prev[k])


def _proj_tile(in_refs, out_refs, prev, *, cs, q_lora, kv_lora, d_model, main=True, gates=True):
    (x_ref, ct_ref, sa_ref, sb_ref, finit_ref, gmix_ref, w_ref, bexp_ref, gq_ref, wuq_ref,
     gkv_ref, wukv_ref, tri_ref) = in_refs
    (fk_ref, fv_ref, logf_ref, ckv_ref, kr_ref, qf_ref, kfa_ref, vf_ref, qn_ref, qr_ref,
     kc_ref, vm_ref, gate_ref, ftot_ref, carry_ref) = out_refs
    c_ckv = C_CQ + q_lora
    c_kr = c_ckv + kv_lora
    c_gate = c_kr + LANES
    xb = _rms(x_ref[0], gmix_ref[...]).astype(BF16)

    def proj(a, b):
        return jnp.dot(xb, w_ref[:, a:b], preferred_element_type=F32)

    def merge_gates():
        gw = 2 * d_model // GATE_CHUNKS
        for c in range(GATE_CHUNKS):
            gate_ref[0, :, c * gw:(c + 1) * gw] = jax.nn.sigmoid(proj(c_gate + c * gw, c_gate + (c + 1) * gw))

    if not main:
        merge_gates()
        return

    ct, sa, sb = ct_ref[...], sa_ref[...], sb_ref[...]
    nope_w = MLA_HEADS * MLA_NOPE
    cq = proj(C_CQ, c_ckv)
    ckv = proj(c_ckv, c_kr)
    ff = proj(C_FF, C_CQ)
    kr = proj(c_kr, c_gate)

    qf_ref[0] = (proj(C_FQ, C_FK) * FOX_QSCALE).T.astype(BF16)
    fk = proj(C_FK, C_FV)
    _store_cache(fk_ref, fk, prev['fk'])
    fv = proj(C_FV, C_FF)
    _store_cache(fv_ref, fv, prev['fv'])
    vf_ref[0] = fv.T.astype(BF16)

    cqn = _rms(cq, gq_ref[...]).astype(BF16)
    ckvn = _rms(ckv, gkv_ref[...])
    _store_cache(ckv_ref, ckvn, prev['ckv'])
    q = jnp.dot(cqn, wuq_ref[...], preferred_element_type=F32) * MLA_QSCALE
    kv = jnp.dot(ckvn.astype(BF16), wukv_ref[...], preferred_element_type=F32)

    lf = jax.nn.log_sigmoid(ff + bexp_ref[...])
    _store_cache(logf_ref, lf[:, :FOX_HEADS], prev['logf'])
    f, carry = _cumsum_rows(lf, tri_ref[...], carry_ref[0:1, :], cs)
    carry_ref[0:1, :] = carry
    ftot_ref[0] = jnp.broadcast_to(carry, ftot_ref.shape[1:])

    if gates:
        merge_gates()

    _store_pairs(kfa_ref, fk.astype(BF16), _forget_columns(f * LOG2E))
    qn_ref[0] = q[:, :nope_w].T.astype(BF16)
    for c in range(MLA_HEADS * MLA_ROPE // LANES):
        qr_ref[0, LANES * c:LANES * (c + 1), :] = _rope(
            q[:, nope_w + LANES * c:nope_w + LANES * (c + 1)], ct, sa, sb).T.astype(BF16)
    vm_ref[0] = kv[:, nope_w:].T.astype(BF16)
    kr4 = _rope(kr, ct, sa, sb)
    _store_cache(kr_ref, kr4[:, :MLA_ROPE], prev['kr'])
    _store_pairs(kc_ref, kv[:, :nope_w].astype(BF16), kr4.astype(BF16))


TRANSPOSED = ('qf', 'vf', 'qn', 'qr', 'vm')
CACHE_KEYS = ('fk', 'fv', 'logf', 'ckv', 'kr')


def _const_spec(shape):
    return pl.BlockSpec(shape, lambda *_: (0,) * len(shape), pipeline_mode=pl.Buffered(1))


def _vmem_limit(block_bytes, const_bytes, temp_bytes):
    need = 2 * block_bytes + const_bytes + temp_bytes
    assert need < VMEM_BYTES_V7X, need
    return int(need)


def _project(x, tables, finit, wts, tri, *, ts, cs, head_rows=None):
    B, n, d = x.shape
    assert n % ts == 0 and ts % cs == 0
    nt = n // ts
    q_lora = wts['g_q'].shape[1]
    kv_lora = wts['g_kv'].shape[1]
    ct, sa, sb = tables
    tile = (lambda t: jnp.minimum(t, nt - 1)) if head_rows is not None else (lambda t: t)
    row = lambda w: pl.BlockSpec((1, ts, w), lambda b, t: (b, tile(t), 0))
    tab = pl.BlockSpec((ts, LANES), lambda b, t: (tile(t), 0))
    in_specs = [row(d), tab, tab, tab,
                pl.BlockSpec((1, 8, LANES), lambda b, t: (0, 0, 0)),
                _const_spec(wts['g_mix'].shape), _const_spec(wts['w_a'].shape), _const_spec(wts['b_exp'].shape),
                _const_spec(wts['g_q'].shape), _const_spec(wts['w_uq'].shape), _const_spec(wts['g_kv'].shape),
                _const_spec(wts['w_ukv'].shape), _const_spec(tri.shape)]
    widths = dict(fk=(FOX_W, F32), fv=(FOX_W, F32), logf=(FOX_HEADS, F32), ckv=(kv_lora, F32), kr=(MLA_ROPE, F32),
                  qf=(FOX_W, BF16), kfa=(N_PAIRS * PAIR_W, BF16), vf=(FOX_W, BF16),
                  qn=(MLA_HEADS * MLA_NOPE, BF16), qr=(MLA_HEADS * MLA_ROPE, BF16),
                  kc=(N_PAIRS * PAIR_W, BF16), vm=(MLA_HEADS * MLA_V, BF16), gate=(2 * d, F32))
    col = lambda w: pl.BlockSpec((1, w, ts), lambda b, t: (b, 0, tile(t)))
    out_shape = [jax.ShapeDtypeStruct((B, w, n) if k in TRANSPOSED else (B, n, w), dt) for k, (w, dt) in widths.items()]
    out_specs = [col(w) if k in TRANSPOSED else row(w) for k, (w, _) in widths.items()]
    args = [x, ct, sa, sb, finit, wts['g_mix'], wts['w_a'], wts['b_exp'], wts['g_q'], wts['w_uq'], wts['g_kv'],
            wts['w_ukv'], tri]
    scratch = [pltpu.VMEM((8, LANES), F32)]
    if head_rows is not None:
        r0 = head_rows[CACHE_KEYS[0]].shape[0]
        assert r0 % 8 == 0 and ts % 8 == 0 and r0 < ts
        for idx, k in enumerate(CACHE_KEYS):
            heads, w = (FOX_HEADS, FOX_HEAD_DIM) if k in ('fk', 'fv') else (1, widths[k][0])
            out_shape[idx] = jax.ShapeDtypeStruct((1, B, (r0 + n) * heads, w), F32)
            out_specs[idx] = pl.BlockSpec((1, 1, ts * heads, w), lambda b, t: (0, b, t, 0))
            args.append(head_rows[k])
            in_specs.append(_const_spec(head_rows[k].shape))
            scratch.append(pltpu.VMEM(head_rows[k].shape, F32))
    out_shape.append(jax.ShapeDtypeStruct((B, 8, LANES), F32))
    out_specs.append(pl.BlockSpec((1, 8, LANES), lambda b, t: (b, 0, 0)))
    block_bytes = ts * (d * 4 + 3 * LANES * 4 + sum(w * jnp.dtype(dt).itemsize for w, dt in widths.values()))
    if head_rows is not None:
        block_bytes += 2 * ts * FOX_HEADS * LANES * 4
    const_bytes = sum(int(np.prod(wts[k].shape)) * wts[k].dtype.itemsize for k in ('w_a', 'w_uq', 'w_ukv')) + tri.size * 2
    temp_bytes = ts * (2 * d + 2 * d) * 4 * 2
    outs = pl.pallas_call(
        functools.partial(_proj_kernel, cs=cs, q_lora=q_lora, kv_lora=kv_lora, d_model=d, nt=nt,
                          with_head_rows=head_rows is not None),
        grid=(B, nt + (head_rows is not None)), in_specs=in_specs, out_specs=out_specs, out_shape=out_shape,
        scratch_shapes=scratch,
        compiler_params=pltpu.CompilerParams(
            dimension_semantics=("arbitrary", "arbitrary"),
            vmem_limit_bytes=_vmem_limit(block_bytes, const_bytes, temp_bytes)),
        name="proj",
    )(*args)
    res = dict(zip(widths.keys(), outs[:-1]))
    res['ftot'] = outs[-1]
    return res


def _prefix_kernel(ckt_ref, cvt_ref, lft_ref, ckv_ref, krt_ref, wukv_ref, tri_ref,
                   kfa_ref, vf_ref, kc_ref, vm_ref, *, cs):
    P = ckt_ref.shape[2]
    lft = lft_ref[0]
    lf_exp = jnp.concatenate([lft] * F_PARTS + [jnp.zeros((LANES - F_PARTS * FOX_HEADS, P), F32)], axis=0).T
    f, total = _cumsum_rows(lf_exp, tri_ref[...], jnp.zeros((1, LANES), F32), cs)
    _store_pairs(kfa_ref, ckt_ref[0].T.astype(BF16), _forget_columns((f - total) * LOG2E))
    vf_ref[0] = cvt_ref[0].astype(BF16)
    kv = jnp.dot(ckv_ref[0].astype(BF16), wukv_ref[...], preferred_element_type=F32)
    nope_w = MLA_HEADS * MLA_NOPE
    vm_ref[0] = kv[:, nope_w:].T.astype(BF16)
    kr4 = jnp.concatenate([krt_ref[0]] * (LANES // MLA_ROPE), axis=0).T
    _store_pairs(kc_ref, kv[:, :nope_w].astype(BF16), kr4.astype(BF16))


def _prefix_operands(ckt, cvt, lft, cckv, krt, w_ukv, tri, *, cs):
    B, _, P = ckt.shape
    row = lambda r, w: pl.BlockSpec((1, r, w), lambda b: (b, 0, 0))
    widths = dict(kfa=N_PAIRS * PAIR_W, vf=FOX_W, kc=N_PAIRS * PAIR_W, vm=MLA_HEADS * MLA_V)
    ins = (ckt, cvt, lft, cckv, krt)
    block_bytes = sum(max(a.shape[1], 8) * max(a.shape[2], LANES) * 4 for a in ins) + P * sum(widths.values()) * 2
    outs = pl.pallas_call(
        functools.partial(_prefix_kernel, cs=cs),
        grid=(B,),
        in_specs=[row(a.shape[1], a.shape[2]) for a in ins] + [_const_spec(w_ukv.shape), _const_spec(tri.shape)],
        out_specs=[row(w, P) if k in TRANSPOSED else row(P, w) for k, w in widths.items()],
        out_shape=[jax.ShapeDtypeStruct((B, w, P) if k in TRANSPOSED else (B, P, w), BF16) for k, w in widths.items()],
        compiler_params=pltpu.CompilerParams(
            dimension_semantics=("arbitrary",),
            vmem_limit_bytes=_vmem_limit(block_bytes, w_ukv.size * 2 + tri.size * 2, P * 1024 * 4 * 4)),
        name="prefix",
    )(*ins, w_ukv, tri)
    return dict(zip(widths.keys(), outs))


def _scores(ks, qts):
    return tuple(jnp.dot(k, qt, preferred_element_type=F32) for k, qt in zip(ks, qts))


def _online_update(states, scores, vts, masks=None, extra=None):
    mids = []
    for c, ((m, l, acc), s) in enumerate(zip(states, scores)):
        if masks is not None:
            s = jnp.where(masks[c], s, NEG)
        n_extra = 0
        if extra is not None:
            n_extra = extra[0][c].shape[0]
            s = jnp.concatenate([extra[0][c], s], axis=0)
        m_new = jnp.maximum(m, jnp.max(s, axis=0, keepdims=True))
        alpha = jnp.exp2(m - m_new)
        p = jnp.exp2(s - m_new)
        mids.append((m_new, alpha * l + jnp.sum(p, axis=0, keepdims=True), alpha * acc, p.astype(BF16), n_extra))
    out = []
    for c, ((m_new, l, acc, p, n_extra), vt) in enumerate(zip(mids, vts)):
        if n_extra:
            acc = acc + jnp.dot(extra[1][c], p[:n_extra], preferred_element_type=F32)
        out.append((m_new, l, acc + jnp.dot(vt, p[n_extra:], preferred_element_type=F32)))
    return tuple(out)


def _attn_kernel(*refs, tq, tk, tkp, nq, pps, has_prefix, n_pref, transpose_out):
    if has_prefix:
        (qf_ref, qn_ref, qr_ref, kfa_ref, vf_ref, kc_ref, vm_ref,
         pkfa_ref, pvf_ref, pkc_ref, pvm_ref, of_ref, om_ref, sa_ref, sb_ref) = refs
    else:
        qf_ref, qn_ref, qr_ref, kfa_ref, vf_ref, kc_ref, vm_ref, of_ref, om_ref, sa_ref, sb_ref = refs
        pkfa_ref = pvf_ref = pkc_ref = pvm_ref = None
    b, j, i = pl.program_id(0), pl.program_id(1), pl.program_id(2)
    pref_bias = None if (not has_prefix or n_pref is None) else jnp.where(b < n_pref, 0.0, NEG)

    frow = lax.broadcasted_iota(jnp.int32, (LANES, tq), 0)
    zero_half = jnp.zeros((FOX_HEAD_DIM, tq), BF16)
    rope_groups = LANES // MLA_ROPE
    rope_rows_per_step = pps * HEADS_PER_PAIR * MLA_ROPE
    chains = []
    for mixer in range(2):
        for p in range(pps):
            for s in range(HEADS_PER_PAIR):
                h = HEADS_PER_PAIR * (pps * j + p) + s
                vrows = slice(p * LANES + s * FOX_HEAD_DIM, p * LANES + (s + 1) * FOX_HEAD_DIM)
                own = [zero_half] * HEADS_PER_PAIR
                own[s] = (qf_ref if mixer == 0 else qn_ref)[0, vrows, :]
                if mixer == 0:
                    extra = jnp.where((frow < F_PARTS * FOX_HEADS) & ((frow & (FOX_HEADS - 1)) == h),
                                      -1.0, 0.0).astype(BF16)
                else:
                    r0 = (p * HEADS_PER_PAIR * MLA_ROPE // LANES) * LANES if rope_rows_per_step > LANES else 0
                    qr = qr_ref[0, r0:r0 + LANES, :]
                    extra = jnp.where((frow >> int(math.log2(MLA_ROPE))) == (h & (rope_groups - 1)), qr, jnp.zeros_like(qr))
                chains.append((jnp.concatenate(own + [extra], axis=0), slice(p * PAIR_W, (p + 1) * PAIR_W), vrows, mixer))
    qts = [c[0] for c in chains]
    key_refs, val_refs = (kfa_ref, kc_ref), (vf_ref, vm_ref)
    pkey_refs, pval_refs = (pkfa_ref, pkc_ref), (pvf_ref, pvm_ref)

    def scores(krefs, rows):
        return _scores([krefs[c[3]][0, rows, c[1]] for c in chains], qts)

    def update(sts, sc, vrefs, rows, **kw):
        return _online_update(sts, sc, [vrefs[c[3]][0, c[2], rows] for c in chains], **kw)

    assert tq == tk
    tile_rows = (lambda t: pl.ds(pl.multiple_of(t * tk, tk), tk)) if nq > 1 else (lambda t: slice(0, tk))

    def put_scores(ref, t):
        for c, s in enumerate(scores(key_refs, tile_rows(t))):
            ref[c] = s

    def step(sts, ref, t, **kw):
        return update(sts, [ref[c] for c in range(len(chains))], val_refs, tile_rows(t), **kw)

    prefix = None
    if has_prefix:
        assert tkp == pkfa_ref.shape[1]
        all_rows = slice(0, tkp)
        psc = scores(pkey_refs, all_rows)
        if pref_bias is not None:
            psc = [s + pref_bias for s in psc]
        prefix = (psc, [pval_refs[c[3]][0, c[2], all_rows] for c in chains])
    put_scores(sa_ref, 0)

    def finish(sts, ref):
        krow = lax.broadcasted_iota(jnp.int32, (tq, tq), 0)
        qcol = lax.broadcasted_iota(jnp.int32, (tq, tq), 1)
        chunk_shift = int(math.log2(CHUNK))
        masks = (krow <= qcol, (krow >> chunk_shift) <= (qcol >> chunk_shift))
        sts = step(sts, ref, i, masks=[masks[c[3]] for c in chains], extra=prefix)
        outs = [acc / l for _, l, acc in sts]
        half = len(outs) // 2
        for o_ref, group in ((of_ref, outs[:half]), (om_ref, outs[half:])):
            ot = jnp.concatenate(group, axis=0)
            o_ref[0] = (ot.T if transpose_out else ot).astype(BF16)

    init = (jnp.full((1, tq), NEG, F32), jnp.zeros((1, tq), F32), jnp.zeros((FOX_HEAD_DIM, tq), F32))
    states = tuple(init for _ in chains)
    if nq == 1:
        finish(states, sa_ref)
    else:
        def body(u, sts):
            put_scores(sb_ref, 2 * u + 1)
            sts = step(sts, sa_ref, 2 * u)
            put_scores(sa_ref, 2 * u + 2)
            return step(sts, sb_ref, 2 * u + 1)

        states = lax.fori_loop(0, i >> 1, body, states)

        @pl.when((i & 1) == 0)
        def _():
            finish(states, sa_ref)

        @pl.when((i & 1) == 1)
        def _():
            put_scores(sb_ref, i)
            finish(step(states, sa_ref, i - 1), sb_ref)


def _attention(ops, prefix, *, tq, tk, tkp, pps, n_pref=None):
    B, n, _ = ops['kfa'].shape
    assert n % tq == 0 and tq % tk == 0 and N_PAIRS % pps == 0
    nq = n // tq
    transpose_out = tq % LANES == 0
    rope_w = max(LANES, pps * HEADS_PER_PAIR * MLA_ROPE)
    steps_per_rope_block = rope_w // (pps * HEADS_PER_PAIR * MLA_ROPE)
    qspec = pl.BlockSpec((1, pps * LANES, tq), lambda b, j, i: (b, j, i))
    qrspec = pl.BlockSpec((1, rope_w, tq), lambda b, j, i: (b, j // steps_per_rope_block, i))
    kspec = pl.BlockSpec((1, n, pps * PAIR_W), lambda b, j, i: (b, 0, j))
    vspec = pl.BlockSpec((1, pps * LANES, n), lambda b, j, i: (b, j, 0))
    args = [ops['qf'], ops['qn'], ops['qr'], ops['kfa'], ops['vf'], ops['kc'], ops['vm']]
    in_specs = [qspec, qspec, qrspec, kspec, vspec, kspec, vspec]
    block_bytes = 2 * pps * (3 * tq * LANES + 2 * n * (PAIR_W + LANES) + 2 * tq * LANES)
    if prefix is not None:
        Bp, P, _ = prefix['kfa'].shape
        assert P % tkp == 0
        if Bp == 1:
            pb = lambda b: 0
        elif n_pref is not None:
            pb = lambda b: jnp.minimum(b, n_pref - 1)
        else:
            pb = lambda b: b
        pk = pl.BlockSpec((1, P, pps * PAIR_W), lambda b, j, i: (pb(b), 0, j))
        pv = pl.BlockSpec((1, pps * LANES, P), lambda b, j, i: (pb(b), j, 0))
        args += [prefix['kfa'], prefix['vf'], prefix['kc'], prefix['vm']]
        in_specs += [pk, pv, pk, pv]
        block_bytes += 2 * 2 * pps * P * (PAIR_W + LANES)
    if transpose_out:
        ospec = pl.BlockSpec((1, tq, pps * LANES), lambda b, j, i: (b, i, j))
        oshape = (B, n, N_PAIRS * LANES)
    else:
        ospec = pl.BlockSpec((1, pps * LANES, tq), lambda b, j, i: (b, j, i))
        oshape = (B, N_PAIRS * LANES, n)
    n_chains = 2 * HEADS_PER_PAIR * pps
    tq_pad = max(tq, LANES)
    temp_bytes = n_chains * 6 * tq_pad * max(tk, tkp if prefix is not None else tk) * 4 + 4 * n_chains * tq_pad * LANES * 4
    of, om = pl.pallas_call(
        functools.partial(_attn_kernel, tq=tq, tk=tk, tkp=tkp, has_prefix=prefix is not None, n_pref=n_pref,
                          nq=nq, pps=pps, transpose_out=transpose_out),
        grid=(B, N_PAIRS // pps, nq), in_specs=in_specs, out_specs=[ospec, ospec],
        out_shape=[jax.ShapeDtypeStruct(oshape, BF16), jax.ShapeDtypeStruct(oshape, BF16)],
        scratch_shapes=[pltpu.VMEM((n_chains, tk, tq), F32), pltpu.VMEM((n_chains, tk, tq), F32)],
        compiler_params=pltpu.CompilerParams(
            dimension_semantics=("arbitrary", "arbitrary", "arbitrary"),
            vmem_limit_bytes=_vmem_limit(block_bytes, 2 * n_chains * tk * tq * 4, temp_bytes)),
        name="attn",
    )(*args)
    if not transpose_out:
        of, om = jnp.swapaxes(of, 1, 2), jnp.swapaxes(om, 1, 2)
    return of, om


def _ffn_kernel(*refs, d_ff, fc, flat_group):
    if flat_group:
        (h_ref, of_ref, om_ref, g_ref, left_ref, wof_ref, wom_ref, wout_ref, gffn_ref, wup_ref,
         cw_ref, cb_ref, wdown_ref, gfin_ref, y_ref, u_ref, l1_ref, l2_ref, _) = refs
        l1_ref[...] = jnp.zeros_like(l1_ref)
        l2_ref[...] = jnp.zeros_like(l2_ref)
        for g in range(left_ref.shape[0]):
            r0 = g * flat_group
            l1_ref[r0:r0 + 1, :] = left_ref[g, 1:2, :]
            l2_ref[r0:r0 + 1, :] = left_ref[g, 0:1, :]
            l2_ref[r0 + 1:r0 + 2, :] = left_ref[g, 1:2, :]
        ub_ref = refs[-1]
        ub_ref[0:HALO, :] = jnp.zeros((HALO, ub_ref.shape[1]), F32)
    else:
        (h_ref, of_ref, om_ref, g_ref, left_ref, wof_ref, wom_ref, wout_ref, gffn_ref, wup_ref,
         cw_ref, cb_ref, wdown_ref, gfin_ref, y_ref, u_ref, ub_ref) = refs

        @pl.when(pl.program_id(1) == 0)
        def _():
            ub_ref[HALO - (CONV_W - 1):HALO, :] = left_ref[0]

    ts, d = h_ref.shape[1], h_ref.shape[2]
    ya = jnp.dot(of_ref[0], wof_ref[...], preferred_element_type=F32)
    yb = jnp.dot(om_ref[0], wom_ref[...], preferred_element_type=F32)
    mix = g_ref[0, :, :d] * ya + g_ref[0, :, d:] * yb
    h1 = h_ref[0] + jnp.dot(mix.astype(BF16), wout_ref[...], preferred_element_type=F32)
    xn = _rms(h1, gffn_ref[...]).astype(BF16)

    if flat_group:
        row = lax.broadcasted_iota(jnp.int32, (ts, 1), 0)
        keep1 = ((row & (flat_group - 1)) >= 1).astype(F32)
        keep2 = ((row & (flat_group - 1)) >= 2).astype(F32)

    def up(c0):
        return jnp.dot(xn, wup_ref[:, c0:c0 + fc], preferred_element_type=F32)

    def conv(u, c0):
        cols = slice(c0, c0 + fc)
        ub_ref[HALO:HALO + ts, cols] = u
        u1 = ub_ref[HALO - 1:HALO - 1 + ts, cols]
        u2 = ub_ref[HALO - 2:HALO - 2 + ts, cols]
        if flat_group:
            u_ref[0, :, cols] = u
            u1 = u1 * keep1 + l1_ref[:, cols]
            u2 = u2 * keep2 + l2_ref[:, cols]
        return cb_ref[:, cols] + u2 * cw_ref[0:1, cols] + u1 * cw_ref[1:2, cols] + u * cw_ref[2:3, cols]

    n_chunks = d_ff // fc
    acc = jnp.zeros((ts, d), F32)
    ups = [(up(c * fc), up(d_ff + c * fc)) for c in range(min(UP_AHEAD, n_chunks))]
    for c in range(n_chunks):
        u_gate, u_val = ups.pop(0)
        if c + UP_AHEAD < n_chunks:
            ups.append((up((c + UP_AHEAD) * fc), up(d_ff + (c + UP_AHEAD) * fc)))
        act = (jax.nn.silu(conv(u_gate, c * fc)) * conv(u_val, d_ff + c * fc)).astype(BF16)
        acc = acc + jnp.dot(act, wdown_ref[c * fc:(c + 1) * fc, :], preferred_element_type=F32)
    y_ref[0] = _rms(h1 + acc, gfin_ref[...])
    if not flat_group:
        tail = ub_ref[HALO + ts - (CONV_W - 1):HALO + ts, :]
        u_ref[0] = tail
        ub_ref[HALO - (CONV_W - 1):HALO, :] = tail


def _merge_ffn(h, of, om, gates, left, wts, *, ts, flat_group=0):
    B, n, d = h.shape
    d_ff = wts['w_down'].shape[0]
    fc = MXU_DIM
    assert n % ts == 0 and d_ff % fc == 0
    row = lambda w: pl.BlockSpec((1, ts, w), lambda b, t: (b, t, 0))
    consts = [wts[k] for k in ('w_o_fox', 'w_o_mla', 'w_out', 'g_ffn', 'w_up', 'conv_w', 'conv_b', 'w_down', 'g_fin')]
    const_specs = [_const_spec(c.shape) for c in consts]
    in_specs = [row(d), row(of.shape[2]), row(om.shape[2]), row(2 * d)]
    args = [h, of, om, gates, left]
    if flat_group:
        assert B == 1 and ts % flat_group == 0 and left.shape[0] * flat_group == n
        gpt = ts // flat_group
        in_specs.append(pl.BlockSpec((gpt, CONV_W - 1, 2 * d_ff), lambda b, t: (t, 0, 0)))
        u_shape, u_spec = (1, n, 2 * d_ff), row(2 * d_ff)
        scratch = [pltpu.VMEM((ts, 2 * d_ff), F32), pltpu.VMEM((ts, 2 * d_ff), F32)]
        const_extra = 2 * ts * 2 * d_ff * 4 + 2 * gpt * 8 * 2 * d_ff * 4
    else:
        bl = left.shape[0]
        in_specs.append(pl.BlockSpec((1, CONV_W - 1, 2 * d_ff), lambda b, t: (b if bl > 1 else 0, 0, 0)))
        u_shape = (B, CONV_W - 1, 2 * d_ff)
        u_spec = pl.BlockSpec((1, CONV_W - 1, 2 * d_ff), lambda b, t: (b, 0, 0))
        scratch = []
        const_extra = 0
    scratch.append(pltpu.VMEM((HALO + ts, 2 * d_ff), F32))
    const_extra += (HALO + ts) * 2 * d_ff * 4
    u_rows = ts if flat_group else 8
    block_bytes = ts * (d * 4 + (of.shape[2] + om.shape[2]) * 2 + 2 * d * 4 + d * 4) + u_rows * 2 * d_ff * 4
    const_bytes = sum(c.size * c.dtype.itemsize for c in consts) + const_extra
    temp_bytes = ts * d * 4 * 8 + ts * fc * 4 * 16
    y, u = pl.pallas_call(
        functools.partial(_ffn_kernel, d_ff=d_ff, fc=fc, flat_group=flat_group),
        grid=(B, n // ts), in_specs=in_specs + const_specs,
        out_specs=[row(d), u_spec],
        out_shape=[jax.ShapeDtypeStruct((B, n, d), F32), jax.ShapeDtypeStruct(u_shape, F32)],
        scratch_shapes=scratch,
        compiler_params=pltpu.CompilerParams(
            dimension_semantics=("arbitrary", "arbitrary"),
            vmem_limit_bytes=_vmem_limit(block_bytes, const_bytes, temp_bytes)),
        name="ffn",
    )(*args, *consts)
    return y, u


def _rope_tables(pos):
    half = MLA_ROPE // 2
    inv = ROPE_BASE ** (-jnp.arange(0, MLA_ROPE, 2, dtype=F32) / MLA_ROPE)
    ang = pos.astype(F32)[:, None] * inv[None, :]
    cos, sin = jnp.cos(ang), jnp.sin(ang)
    zero = jnp.zeros_like(sin)
    reps = LANES // MLA_ROPE
    ct = jnp.tile(jnp.concatenate([cos, cos], axis=1), (1, reps))
    sa = jnp.tile(jnp.concatenate([-sin, zero], axis=1), (1, reps))
    sb = jnp.tile(jnp.concatenate([zero, sin], axis=1), (1, reps))
    assert half * 2 == MLA_ROPE
    return ct, sa, sb


def _lower_tri(n, group):
    i = np.arange(n)
    m = (i[None, :] <= i[:, None]) & ((i[None, :] // group) == (i[:, None] // group))
    return jnp.asarray(m, BF16)


def _prepare_weights(norm_mix_g, w_in, b_forget, mla_q_norm_g, w_uq, mla_kv_norm_g, w_ukv, w_o_fox, w_o_mla, w_out,
                     norm_ffn_g, w_up, conv_w, conv_b, w_down, norm_final_g):
    d = w_in.shape[0]
    q_lora, kv_lora = mla_q_norm_g.shape[0], mla_kv_norm_g.shape[0]
    off_ff = 3 * FOX_W
    off_cq = off_ff + FOX_HEADS
    off_ckv = off_cq + q_lora
    off_kr = off_ckv + kv_lora
    off_gate = off_kr + MLA_ROPE
    pad = LANES - F_PARTS * FOX_HEADS
    w_ff = jnp.pad(jnp.tile(w_in[:, off_ff:off_cq], (1, F_PARTS)), ((0, 0), (0, pad)))
    w_a = jnp.concatenate([
        w_in[:, :off_ff], w_ff, w_in[:, off_cq:off_kr],
        jnp.tile(w_in[:, off_kr:off_gate], (1, LANES // MLA_ROPE)),
        w_in[:, off_gate:]], axis=1).astype(BF16)
    uq = w_uq.reshape(q_lora, MLA_HEADS, MLA_QK)
    w_uq_p = jnp.concatenate([uq[:, :, :MLA_NOPE].reshape(q_lora, -1), uq[:, :, MLA_NOPE:].reshape(q_lora, -1)], axis=1)
    ukv = w_ukv.reshape(kv_lora, MLA_HEADS, MLA_NOPE + MLA_V)
    w_ukv_p = jnp.concatenate([ukv[:, :, :MLA_NOPE].reshape(kv_lora, -1), ukv[:, :, MLA_NOPE:].reshape(kv_lora, -1)], axis=1)
    return dict(
        g_mix=norm_mix_g.reshape(1, d), w_a=w_a,
        b_exp=jnp.pad(jnp.tile(b_forget, F_PARTS), (0, pad)).reshape(1, LANES),
        g_q=mla_q_norm_g.reshape(1, q_lora), w_uq=w_uq_p.astype(BF16),
        g_kv=mla_kv_norm_g.reshape(1, kv_lora), w_ukv=w_ukv_p.astype(BF16),
        w_o_fox=w_o_fox.astype(BF16), w_o_mla=w_o_mla.astype(BF16), w_out=w_out.astype(BF16),
        g_ffn=norm_ffn_g.reshape(1, d), w_up=w_up.astype(BF16), conv_w=conv_w, conv_b=conv_b.reshape(1, -1),
        w_down=w_down.astype(BF16), g_fin=norm_final_g.reshape(1, d))


PROJ_ROWS = 512
FFN_ROWS = 256
PROMPT_TQ = 256
PROMPT_PAIRS = 2
SMALL_ROWS = LANES
SMALL_PAIRS = 2


def kernel(x_prompt, x_sample, cache_fox_k, cache_fox_v, cache_fox_logf, cache_mla_ckv, cache_mla_krope, state_ffn_conv, meta_tokens, norm_mix_g, w_in, b_forget, mla_q_norm_g, w_uq, mla_kv_norm_g, w_ukv, w_o_fox, w_o_mla, w_out, norm_ffn_g, w_up, conv_w, conv_b, w_down, norm_final_g):
    assert w_in.shape[0] == 1, "single-layer model"
    B, S, d = x_prompt.shape
    Bs, ns, _ = x_sample.shape
    P = cache_fox_k.shape[2]
    n_meta = meta_tokens.shape[0]
    assert n_meta == N_META == ns
    wts = _prepare_weights(norm_mix_g[0], w_in[0], b_forget[0], mla_q_norm_g[0], w_uq[0], mla_kv_norm_g[0], w_ukv[0],
                           w_o_fox[0], w_o_mla[0], w_out[0], norm_ffn_g[0], w_up[0], conv_w[0], conv_b[0], w_down[0],
                           norm_final_g)
    d_ff2 = w_up.shape[2]

    r_pad = -(-(Bs + 1) * ns // SMALL_ROWS) * SMALL_ROWS
    n_groups = r_pad // ns
    n_fill = r_pad - (Bs + 1) * ns
    rows = jnp.concatenate([x_sample.reshape(Bs * ns, d), jnp.zeros((n_fill, d), x_sample.dtype),
                            meta_tokens.astype(x_sample.dtype)], axis=0)[None]
    pos_small = jnp.concatenate([jnp.tile(P + jnp.arange(ns), Bs), jnp.zeros((n_fill,), jnp.int32), jnp.arange(n_meta)])
    zero_f = jnp.zeros((1, 8, LANES), F32)
    sm = _project(rows, _rope_tables(pos_small), zero_f, wts, _lower_tri(r_pad, ns), ts=r_pad, cs=r_pad)
    sm_b = {k: (jnp.swapaxes(v.reshape(v.shape[1], n_groups, ns), 0, 1) if k in TRANSPOSED
                else v.reshape(n_groups, ns, v.shape[-1])) for k, v in sm.items() if k != 'ftot'}
    live = lambda a: jnp.concatenate([a[:Bs], a[n_groups - 1:]], axis=0)

    feat_major = lambda a: jnp.moveaxis(a[0], 1, -1).reshape(Bs, -1, P)
    prefix_s = _prefix_operands(
        feat_major(cache_fox_k), feat_major(cache_fox_v), feat_major(cache_fox_logf), cache_mla_ckv[0],
        feat_major(cache_mla_krope), wts['w_ukv'], _lower_tri(MXU_DIM, MXU_DIM), cs=MXU_DIM)
    of_s, om_s = _attention({k: live(v) for k, v in sm_b.items()}, prefix_s, tq=ns, tk=ns, tkp=P, pps=SMALL_PAIRS, n_pref=Bs)

    def spread(a):
        return jnp.concatenate([a[:Bs].reshape(Bs * ns, -1), jnp.zeros((n_fill, a.shape[-1]), a.dtype), a[Bs]], axis=0)[None]

    left_groups = jnp.concatenate([state_ffn_conv[0], jnp.zeros((n_groups - Bs, CONV_W - 1, d_ff2), F32)], axis=0)
    y_s, u_s = _merge_ffn(rows, spread(of_s), spread(om_s), sm['gate'], left_groups, wts, ts=SMALL_ROWS, flat_group=ns)
    y_sample = y_s[0, :Bs * ns].reshape(Bs, ns, d)
    u_groups = u_s.reshape(n_groups, ns, d_ff2)
    conv_s = u_groups[:Bs, ns - (CONV_W - 1):]
    left_meta = u_groups[n_groups - 1:, ns - (CONV_W - 1):]

    meta_ops = {k: sm_b[k][n_groups - 1:] for k in ('kfa', 'vf', 'kc', 'vm')}
    pos_p = n_meta + jnp.arange(S)
    pr = _project(x_prompt, _rope_tables(pos_p), sm['ftot'], wts, _lower_tri(MXU_DIM, MXU_DIM), ts=PROJ_ROWS,
                  cs=MXU_DIM, head_rows={k: sm_b[k][n_groups - 1] for k in CACHE_KEYS})
    of_p, om_p = _attention(pr, meta_ops, tq=PROMPT_TQ, tk=PROMPT_TQ, tkp=n_meta, pps=PROMPT_PAIRS)
    y_prompt, conv_p = _merge_ffn(x_prompt, of_p, om_p, pr['gate'], left_meta, wts, ts=FFN_ROWS)
    caches_p = [pr[k].reshape((1, B, n_meta + S) + ((FOX_HEADS, FOX_HEAD_DIM) if k in ('fk', 'fv') else pr[k].shape[3:]))
                for k in CACHE_KEYS]

    def sample_rows(name, tail):
        return sm_b[name][:Bs].reshape((1, Bs, ns) + tail)

    hd = (FOX_HEADS, FOX_HEAD_DIM)
    return (y_prompt, y_sample, *caches_p, conv_p[None],
            sample_rows('fk', hd), sample_rows('fv', hd), sample_rows('logf', (FOX_HEADS,)),
            sample_rows('ckv', (cache_mla_ckv.shape[-1],)), sample_rows('kr', (MLA_ROPE,)), conv_s[None])
```

```python
import functools
import math

import jax
import jax.numpy as jnp
import numpy as np
from jax import lax
from jax.experimental import pallas as pl
from jax.experimental.pallas import tpu as pltpu

F32 = jnp.float32
BF16 = jnp.bfloat16

N_META = 16
CHUNK = 64
EPS = 1e-6
NEG = -1e30
FOX_HEADS = 8
FOX_HEAD_DIM = 64
MLA_HEADS = 8
MLA_NOPE = 64
MLA_ROPE = 32
MLA_V = 64
MLA_QK = MLA_NOPE + MLA_ROPE
ROPE_BASE = 10000.0
CONV_W = 3

FOX_W = FOX_HEADS * FOX_HEAD_DIM
LOG2E = math.log2(math.e)
FOX_QSCALE = LOG2E / math.sqrt(FOX_HEAD_DIM)
MLA_QSCALE = LOG2E / math.sqrt(MLA_QK)

LANES = 128
MXU_DIM = 256
VMEM_BYTES_V7X = 64 * 1024 * 1024
F_PARTS = 3
HEADS_PER_PAIR = LANES // FOX_HEAD_DIM
N_PAIRS = FOX_HEADS // HEADS_PER_PAIR
PAIR_W = 2 * LANES
HALO = 8
GATE_CHUNKS = 4
UP_AHEAD = 3

C_FQ = 0
C_FK = C_FQ + FOX_W
C_FV = C_FK + FOX_W
C_FF = C_FV + FOX_W
C_CQ = C_FF + LANES


def _rms(x, g):
    return x * lax.rsqrt(jnp.mean(x * x, axis=-1, keepdims=True) + EPS) * g


def _split3(x):
    hi = x.astype(BF16)
    r = x - hi.astype(F32)
    mid = r.astype(BF16)
    lo = (r - mid.astype(F32)).astype(BF16)
    return hi, mid, lo


def _cumsum_rows(lf, tri, carry, cs):
    hi, mid, lo = _split3(lf)
    pieces = []
    for c in range(lf.shape[0] // cs):
        sl = slice(c * cs, (c + 1) * cs)
        fc = (jnp.dot(tri, hi[sl], preferred_element_type=F32)
              + jnp.dot(tri, mid[sl], preferred_element_type=F32)
              + jnp.dot(tri, lo[sl], preferred_element_type=F32)) + carry
        carry = fc[cs - 1:cs, :]
        pieces.append(fc)
    f = pieces[0] if len(pieces) == 1 else jnp.concatenate(pieces, axis=0)
    return f, carry


def _forget_columns(f):
    hi, mid, lo = _split3(f)
    lane = lax.broadcasted_iota(jnp.int32, f.shape, 1)
    zero = jnp.zeros_like(hi)
    return jnp.where(lane < FOX_HEADS, hi,
                     jnp.where(lane < 2 * FOX_HEADS, mid,
                               jnp.where(lane < F_PARTS * FOX_HEADS, lo, zero)))


def _store_pairs(ref, main, extra):
    for j in range(N_PAIRS):
        ref[0, :, PAIR_W * j:PAIR_W * j + LANES] = main[:, LANES * j:LANES * (j + 1)]
        ref[0, :, PAIR_W * j + LANES:PAIR_W * (j + 1)] = extra


def _store_cache(ref, val, prev_ref=None):
    lead = (0,) * (len(ref.shape) - 2)
    width = prev_ref.shape[1] if val is None else val.shape[1]
    heads = width // ref.shape[-1]

    def put(row0, v):
        rows = v.shape[0]
        if heads == 1:
            ref[lead + (slice(row0, row0 + rows), slice(None))] = v
        else:
            dh = ref.shape[-1]
            for h in range(heads):
                ref[lead + (pl.ds(row0 * heads + h, rows, stride=heads), slice(None))] = v[:, h * dh:(h + 1) * dh]

    if prev_ref is None:
        assert ref.shape[-2] == val.shape[0] * heads
        put(0, val)
        return
    r0 = prev_ref.shape[0]
    put(0, prev_ref[...])
    if val is not None:
        assert ref.shape[-2] == val.shape[0] * heads
        put(r0, val[:val.shape[0] - r0, :])
        prev_ref[...] = val[val.shape[0] - r0:, :]


def _rope(x, ct, sa, sb):
    return x * ct + pltpu.roll(x, LANES - MLA_ROPE // 2, 1) * sa + pltpu.roll(x, MLA_ROPE // 2, 1) * sb


def _proj_kernel(*refs, nt, with_head_rows, **kw):
    n_in = 13 + (len(CACHE_KEYS) if with_head_rows else 0)
    cache_refs = refs[n_in:n_in + len(CACHE_KEYS)]
    carry_ref = refs[n_in + 14]
    prev = dict(zip(CACHE_KEYS, refs[n_in + 15:])) if with_head_rows else dict.fromkeys(CACHE_KEYS)
    t = pl.program_id(1)

    @pl.when(t == 0)
    def _():
        carry_ref[...] = refs[4][0]
        if with_head_rows:
            for k, head_ref in zip(CACHE_KEYS, refs[13:n_in]):
                prev[k][...] = head_ref[...]

    tile_refs = (refs[:13], refs[n_in:n_in + 15], prev)
    if not with_head_rows:
        _proj_tile(*tile_refs, **kw)
        return

    @pl.when(t < nt - 1)
    def _():
        _proj_tile(*tile_refs, **kw)

    @pl.when(t == nt - 1)
    def _():
        _proj_tile(*tile_refs, gates=False, **kw)

    @pl.when(t == nt)
    def _():
        _proj_tile(*tile_refs, main=False, **kw)
        for k, ref in zip(CACHE_KEYS, cache_refs):
            _store_cache(ref, None, prev[k])


def _proj_tile(in_refs, out_refs, prev, *, cs, q_lora, kv_lora, d_model, main=True, gates=True):
    (x_ref, ct_ref, sa_ref, sb_ref, finit_ref, gmix_ref, w_ref, bexp_ref, gq_ref, wuq_ref,
     gkv_ref, wukv_ref, tri_ref) = in_refs
    (fk_ref, fv_ref, logf_ref, ckv_ref, kr_ref, qf_ref, kfa_ref, vf_ref, qn_ref, qr_ref,
     kc_ref, vm_ref, gate_ref, ftot_ref, carry_ref) = out_refs
    c_ckv = C_CQ + q_lora
    c_kr = c_ckv + kv_lora
    c_gate = c_kr + LANES
    xb = _rms(x_ref[0], gmix_ref[...]).astype(BF16)

    def proj(a, b):
        return jnp.dot(xb, w_ref[:, a:b], preferred_element_type=F32)

    def merge_gates():
        gw = 2 * d_model // GATE_CHUNKS
        for c in range(GATE_CHUNKS):
            gate_ref[0, :, c * gw:(c + 1) * gw] = jax.nn.sigmoid(proj(c_gate + c * gw, c_gate + (c + 1) * gw))

    if not main:
        merge_gates()
        return

    ct, sa, sb = ct_ref[...], sa_ref[...], sb_ref[...]
    nope_w = MLA_HEADS * MLA_NOPE
    cq = proj(C_CQ, c_ckv)
    ckv = proj(c_ckv, c_kr)
    ff = proj(C_FF, C_CQ)
    kr = proj(c_kr, c_gate)

    qf_ref[0] = (proj(C_FQ, C_FK) * FOX_QSCALE).T.astype(BF16)
    fk = proj(C_FK, C_FV)
    _store_cache(fk_ref, fk, prev['fk'])
    fv = proj(C_FV, C_FF)
    _store_cache(fv_ref, fv, prev['fv'])
    vf_ref[0] = fv.T.astype(BF16)

    cqn = _rms(cq, gq_ref[...]).astype(BF16)
    ckvn = _rms(ckv, gkv_ref[...])
    _store_cache(ckv_ref, ckvn, prev['ckv'])
    q = jnp.dot(cqn, wuq_ref[...], preferred_element_type=F32) * MLA_QSCALE
    kv = jnp.dot(ckvn.astype(BF16), wukv_ref[...], preferred_element_type=F32)

    lf = jax.nn.log_sigmoid(ff + bexp_ref[...])
    _store_cache(logf_ref, lf[:, :FOX_HEADS], prev['logf'])
    f, carry = _cumsum_rows(lf, tri_ref[...], carry_ref[0:1, :], cs)
    carry_ref[0:1, :] = carry
    ftot_ref[0] = jnp.broadcast_to(carry, ftot_ref.shape[1:])

    if gates:
        merge_gates()

    _store_pairs(kfa_ref, fk.astype(BF16), _forget_columns(f * LOG2E))
    qn_ref[0] = q[:, :nope_w].T.astype(BF16)
    for c in range(MLA_HEADS * MLA_ROPE // LANES):
        qr_ref[0, LANES * c:LANES * (c + 1), :] = _rope(
            q[:, nope_w + LANES * c:nope_w + LANES * (c + 1)], ct, sa, sb).T.astype(BF16)
    vm_ref[0] = kv[:, nope_w:].T.astype(BF16)
    kr4 = _rope(kr, ct, sa, sb)
    _store_cache(kr_ref, kr4[:, :MLA_ROPE], prev['kr'])
    _store_pairs(kc_ref, kv[:, :nope_w].astype(BF16), kr4.astype(BF16))


TRANSPOSED = ('qf', 'vf', 'qn', 'qr', 'vm')
CACHE_KEYS = ('fk', 'fv', 'logf', 'ckv', 'kr')


def _const_spec(shape):
    return pl.BlockSpec(shape, lambda *_: (0,) * len(shape), pipeline_mode=pl.Buffered(1))


def _vmem_limit(block_bytes, const_bytes, temp_bytes):
    need = 2 * block_bytes + const_bytes + temp_bytes
    assert need < VMEM_BYTES_V7X, need
    return int(need)


def _project(x, tables, finit, wts, tri, *, ts, cs, head_rows=None):
    B, n, d = x.shape
    assert n % ts == 0 and ts % cs == 0
    nt = n // ts
    q_lora = wts['g_q'].shape[1]
    kv_lora = wts['g_kv'].shape[1]
    ct, sa, sb = tables
    tile = (lambda t: jnp.minimum(t, nt - 1)) if head_rows is not None else (lambda t: t)
    row = lambda w: pl.BlockSpec((1, ts, w), lambda b, t: (b, tile(t), 0))
    tab = pl.BlockSpec((ts, LANES), lambda b, t: (tile(t), 0))
    in_specs = [row(d), tab, tab, tab,
                pl.BlockSpec((1, 8, LANES), lambda b, t: (0, 0, 0)),
                _const_spec(wts['g_mix'].shape), _const_spec(wts['w_a'].shape), _const_spec(wts['b_exp'].shape),
                _const_spec(wts['g_q'].shape), _const_spec(wts['w_uq'].shape), _const_spec(wts['g_kv'].shape),
                _const_spec(wts['w_ukv'].shape), _const_spec(tri.shape)]
    widths = dict(fk=(FOX_W, F32), fv=(FOX_W, F32), logf=(FOX_HEADS, F32), ckv=(kv_lora, F32), kr=(MLA_ROPE, F32),
                  qf=(FOX_W, BF16), kfa=(N_PAIRS * PAIR_W, BF16), vf=(FOX_W, BF16),
                  qn=(MLA_HEADS * MLA_NOPE, BF16), qr=(MLA_HEADS * MLA_ROPE, BF16),
                  kc=(N_PAIRS * PAIR_W, BF16), vm=(MLA_HEADS * MLA_V, BF16), gate=(2 * d, F32))
    col = lambda w: pl.BlockSpec((1, w, ts), lambda b, t: (b, 0, tile(t)))
    out_shape = [jax.ShapeDtypeStruct((B, w, n) if k in TRANSPOSED else (B, n, w), dt) for k, (w, dt) in widths.items()]
    out_specs = [col(w) if k in TRANSPOSED else row(w) for k, (w, _) in widths.items()]
    args = [x, ct, sa, sb, finit, wts['g_mix'], wts['w_a'], wts['b_exp'], wts['g_q'], wts['w_uq'], wts['g_kv'],
            wts['w_ukv'], tri]
    scratch = [pltpu.VMEM((8, LANES), F32)]
    if head_rows is not None:
        r0 = head_rows[CACHE_KEYS[0]].shape[0]
        assert r0 % 8 == 0 and ts % 8 == 0 and r0 < ts
        for idx, k in enumerate(CACHE_KEYS):
            heads, w = (FOX_HEADS, FOX_HEAD_DIM) if k in ('fk', 'fv') else (1, widths[k][0])
            out_shape[idx] = jax.ShapeDtypeStruct((1, B, (r0 + n) * heads, w), F32)
            out_specs[idx] = pl.BlockSpec((1, 1, ts * heads, w), lambda b, t: (0, b, t, 0))
            args.append(head_rows[k])
            in_specs.append(_const_spec(head_rows[k].shape))
            scratch.append(pltpu.VMEM(head_rows[k].shape, F32))
    out_shape.append(jax.ShapeDtypeStruct((B, 8, LANES), F32))
    out_specs.append(pl.BlockSpec((1, 8, LANES), lambda b, t: (b, 0, 0)))
    block_bytes = ts * (d * 4 + 3 * LANES * 4 + sum(w * jnp.dtype(dt).itemsize for w, dt in widths.values()))
    if head_rows is not None:
        block_bytes += 2 * ts * FOX_HEADS * LANES * 4
    const_bytes = sum(int(np.prod(wts[k].shape)) * wts[k].dtype.itemsize for k in ('w_a', 'w_uq', 'w_ukv')) + tri.size * 2
    temp_bytes = ts * (2 * d + 2 * d) * 4 * 2
    outs = pl.pallas_call(
        functools.partial(_proj_kernel, cs=cs, q_lora=q_lora, kv_lora=kv_lora, d_model=d, nt=nt,
                          with_head_rows=head_rows is not None),
        grid=(B, nt + (head_rows is not None)), in_specs=in_specs, out_specs=out_specs, out_shape=out_shape,
        scratch_shapes=scratch,
        compiler_params=pltpu.CompilerParams(
            dimension_semantics=("arbitrary", "arbitrary"),
            vmem_limit_bytes=_vmem_limit(block_bytes, const_bytes, temp_bytes)),
        name="proj",
    )(*args)
    res = dict(zip(widths.keys(), outs[:-1]))
    res['ftot'] = outs[-1]
    return res


def _prefix_kernel(ckt_ref, cvt_ref, lft_ref, ckv_ref, krt_ref, wukv_ref, tri_ref,
                   kfa_ref, vf_ref, kc_ref, vm_ref, *, cs):
    P = ckt_ref.shape[2]
    lft = lft_ref[0]
    lf_exp = jnp.concatenate([lft] * F_PARTS + [jnp.zeros((LANES - F_PARTS * FOX_HEADS, P), F32)], axis=0).T
    f, total = _cumsum_rows(lf_exp, tri_ref[...], jnp.zeros((1, LANES), F32), cs)
    _store_pairs(kfa_ref, ckt_ref[0].T.astype(BF16), _forget_columns((f - total) * LOG2E))
    vf_ref[0] = cvt_ref[0].astype(BF16)
    kv = jnp.dot(ckv_ref[0].astype(BF16), wukv_ref[...], preferred_element_type=F32)
    nope_w = MLA_HEADS * MLA_NOPE
    vm_ref[0] = kv[:, nope_w:].T.astype(BF16)
    kr4 = jnp.concatenate([krt_ref[0]] * (LANES // MLA_ROPE), axis=0).T
    _store_pairs(kc_ref, kv[:, :nope_w].astype(BF16), kr4.astype(BF16))


def _prefix_operands(ckt, cvt, lft, cckv, krt, w_ukv, tri, *, cs):
    B, _, P = ckt.shape
    row = lambda r, w: pl.BlockSpec((1, r, w), lambda b: (b, 0, 0))
    widths = dict(kfa=N_PAIRS * PAIR_W, vf=FOX_W, kc=N_PAIRS * PAIR_W, vm=MLA_HEADS * MLA_V)
    ins = (ckt, cvt, lft, cckv, krt)
    block_bytes = sum(max(a.shape[1], 8) * max(a.shape[2], LANES) * 4 for a in ins) + P * sum(widths.values()) * 2
    outs = pl.pallas_call(
        functools.partial(_prefix_kernel, cs=cs),
        grid=(B,),
        in_specs=[row(a.shape[1], a.shape[2]) for a in ins] + [_const_spec(w_ukv.shape), _const_spec(tri.shape)],
        out_specs=[row(w, P) if k in TRANSPOSED else row(P, w) for k, w in widths.items()],
        out_shape=[jax.ShapeDtypeStruct((B, w, P) if k in TRANSPOSED else (B, P, w), BF16) for k, w in widths.items()],
        compiler_params=pltpu.CompilerParams(
            dimension_semantics=("arbitrary",),
            vmem_limit_bytes=_vmem_limit(block_bytes, w_ukv.size * 2 + tri.size * 2, P * 1024 * 4 * 4)),
        name="prefix",
    )(*ins, w_ukv, tri)
    return dict(zip(widths.keys(), outs))


def _scores(ks, qts):
    return tuple(jnp.dot(k, qt, preferred_element_type=F32) for k, qt in zip(ks, qts))


def _online_update(states, scores, vts, masks=None, extra=None):
    mids = []
    for c, ((m, l, acc), s) in enumerate(zip(states, scores)):
        if masks is not None:
            s = jnp.where(masks[c], s, NEG)
        n_extra = 0
        if extra is not None:
            n_extra = extra[0][c].shape[0]
            s = jnp.concatenate([extra[0][c], s], axis=0)
        m_new = jnp.maximum(m, jnp.max(s, axis=0, keepdims=True))
        alpha = jnp.exp2(m - m_new)
        p = jnp.exp2(s - m_new)
        mids.append((m_new, alpha * l + jnp.sum(p, axis=0, keepdims=True), alpha * acc, p.astype(BF16), n_extra))
    out = []
    for c, ((m_new, l, acc, p, n_extra), vt) in enumerate(zip(mids, vts)):
        if n_extra:
            acc = acc + jnp.dot(extra[1][c], p[:n_extra], preferred_element_type=F32)
        out.append((m_new, l, acc + jnp.dot(vt, p[n_extra:], preferred_element_type=F32)))
    return tuple(out)


def _attn_kernel(*refs, tq, tk, tkp, nq, pps, has_prefix, n_pref, transpose_out):
    if has_prefix:
        (qf_ref, qn_ref, qr_ref, kfa_ref, vf_ref, kc_ref, vm_ref,
         pkfa_ref, pvf_ref, pkc_ref, pvm_ref, of_ref, om_ref, sa_ref, sb_ref) = refs
    else:
        qf_ref, qn_ref, qr_ref, kfa_ref, vf_ref, kc_ref, vm_ref, of_ref, om_ref, sa_ref, sb_ref = refs
        pkfa_ref = pvf_ref = pkc_ref = pvm_ref = None
    b, j, i = pl.program_id(0), pl.program_id(1), pl.program_id(2)
    pref_bias = None if (not has_prefix or n_pref is None) else jnp.where(b < n_pref, 0.0, NEG)

    frow = lax.broadcasted_iota(jnp.int32, (LANES, tq), 0)
    zero_half = jnp.zeros((FOX_HEAD_DIM, tq), BF16)
    rope_groups = LANES // MLA_ROPE
    rope_rows_per_step = pps * HEADS_PER_PAIR * MLA_ROPE
    chains = []
    for mixer in range(2):
        for p in range(pps):
            for s in range(HEADS_PER_PAIR):
                h = HEADS_PER_PAIR * (pps * j + p) + s
                vrows = slice(p * LANES + s * FOX_HEAD_DIM, p * LANES + (s + 1) * FOX_HEAD_DIM)
                own = [zero_half] * HEADS_PER_PAIR
                own[s] = (qf_ref if mixer == 0 else qn_ref)[0, vrows, :]
                if mixer == 0:
                    extra = jnp.where((frow < F_PARTS * FOX_HEADS) & ((frow & (FOX_HEADS - 1)) == h),
                                      -1.0, 0.0).astype(BF16)
                else:
                    r0 = (p * HEADS_PER_PAIR * MLA_ROPE // LANES) * LANES if rope_rows_per_step > LANES else 0
                    qr = qr_ref[0, r0:r0 + LANES, :]
                    extra = jnp.where((frow >> int(math.log2(MLA_ROPE))) == (h & (rope_groups - 1)), qr, jnp.zeros_like(qr))
                chains.append((jnp.concatenate(own + [extra], axis=0), slice(p * PAIR_W, (p + 1) * PAIR_W), vrows, mixer))
    qts = [c[0] for c in chains]
    key_refs, val_refs = (kfa_ref, kc_ref), (vf_ref, vm_ref)
    pkey_refs, pval_refs = (pkfa_ref, pkc_ref), (pvf_ref, pvm_ref)

    def scores(krefs, rows):
        return _scores([krefs[c[3]][0, rows, c[1]] for c in chains], qts)

    def update(sts, sc, vrefs, rows, **kw):
        return _online_update(sts, sc, [vrefs[c[3]][0, c[2], rows] for c in chains], **kw)

    assert tq == tk
    tile_rows = (lambda t: pl.ds(pl.multiple_of(t * tk, tk), tk)) if nq > 1 else (lambda t: slice(0, tk))

    def put_scores(ref, t):
        for c, s in enumerate(scores(key_refs, tile_rows(t))):
            ref[c] = s

    def step(sts, ref, t, **kw):
        return update(sts, [ref[c] for c in range(len(chains))], val_refs, tile_rows(t), **kw)

    prefix = None
    if has_prefix:
        assert tkp == pkfa_ref.shape[1]
        all_rows = slice(0, tkp)
        psc = scores(pkey_refs, all_rows)
        if pref_bias is not None:
            psc = [s + pref_bias for s in psc]
        prefix = (psc, [pval_refs[c[3]][0, c[2], all_rows] for c in chains])
    put_scores(sa_ref, 0)

    def finish(sts, ref):
        krow = lax.broadcasted_iota(jnp.int32, (tq, tq), 0)
        qcol = lax.broadcasted_iota(jnp.int32, (tq, tq), 1)
        chunk_shift = int(math.log2(CHUNK))
        masks = (krow <= qcol, (krow >> chunk_shift) <= (qcol >> chunk_shift))
        sts = step(sts, ref, i, masks=[masks[c[3]] for c in chains], extra=prefix)
        outs = [acc / l for _, l, acc in sts]
        half = len(outs) // 2
        for o_ref, group in ((of_ref, outs[:half]), (om_ref, outs[half:])):
            ot = jnp.concatenate(group, axis=0)
            o_ref[0] = (ot.T if transpose_out else ot).astype(BF16)

    init = (jnp.full((1, tq), NEG, F32), jnp.zeros((1, tq), F32), jnp.zeros((FOX_HEAD_DIM, tq), F32))
    states = tuple(init for _ in chains)
    if nq == 1:
        finish(states, sa_ref)
    else:
        def body(u, sts):
            put_scores(sb_ref, 2 * u + 1)
            sts = step(sts, sa_ref, 2 * u)
            put_scores(sa_ref, 2 * u + 2)
            return step(sts, sb_ref, 2 * u + 1)

        states = lax.fori_loop(0, i >> 1, body, states)

        @pl.when((i & 1) == 0)
        def _():
            finish(states, sa_ref)

        @pl.when((i & 1) == 1)
        def _():
            put_scores(sb_ref, i)
            finish(step(states, sa_ref, i - 1), sb_ref)


def _attention(ops, prefix, *, tq, tk, tkp, pps, n_pref=None):
    B, n, _ = ops['kfa'].shape
    assert n % tq == 0 and tq % tk == 0 and N_PAIRS % pps == 0
    nq = n // tq
    transpose_out = tq % LANES == 0
    rope_w = max(LANES, pps * HEADS_PER_PAIR * MLA_ROPE)
    steps_per_rope_block = rope_w // (pps * HEADS_PER_PAIR * MLA_ROPE)
    qspec = pl.BlockSpec((1, pps * LANES, tq), lambda b, j, i: (b, j, i))
    qrspec = pl.BlockSpec((1, rope_w, tq), lambda b, j, i: (b, j // steps_per_rope_block, i))
    kspec = pl.BlockSpec((1, n, pps * PAIR_W), lambda b, j, i: (b, 0, j))
    vspec = pl.BlockSpec((1, pps * LANES, n), lambda b, j, i: (b, j, 0))
    args = [ops['qf'], ops['qn'], ops['qr'], ops['kfa'], ops['vf'], ops['kc'], ops['vm']]
    in_specs = [qspec, qspec, qrspec, kspec, vspec, kspec, vspec]
    block_bytes = 2 * pps * (3 * tq * LANES + 2 * n * (PAIR_W + LANES) + 2 * tq * LANES)
    if prefix is not None:
        Bp, P, _ = prefix['kfa'].shape
        assert P % tkp == 0
        if Bp == 1:
            pb = lambda b: 0
        elif n_pref is not None:
            pb = lambda b: jnp.minimum(b, n_pref - 1)
        else:
            pb = lambda b: b
        pk = pl.BlockSpec((1, P, pps * PAIR_W), lambda b, j, i: (pb(b), 0, j))
        pv = pl.BlockSpec((1, pps * LANES, P), lambda b, j, i: (pb(b), j, 0))
        args += [prefix['kfa'], prefix['vf'], prefix['kc'], prefix['vm']]
        in_specs += [pk, pv, pk, pv]
        block_bytes += 2 * 2 * pps * P * (PAIR_W + LANES)
    if transpose_out:
        ospec = pl.BlockSpec((1, tq, pps * LANES), lambda b, j, i: (b, i, j))
        oshape = (B, n, N_PAIRS * LANES)
    else:
        ospec = pl.BlockSpec((1, pps * LANES, tq), lambda b, j, i: (b, j, i))
        oshape = (B, N_PAIRS * LANES, n)
    n_chains = 2 * HEADS_PER_PAIR * pps
    tq_pad = max(tq, LANES)
    temp_bytes = n_chains * 3 * tq_pad * max(tk, tkp if prefix is not None else tk) * 4 + 4 * n_chains * tq_pad * LANES * 4
    of, om = pl.pallas_call(
        functools.partial(_attn_kernel, tq=tq, tk=tk, tkp=tkp, has_prefix=prefix is not None, n_pref=n_pref,
                          nq=nq, pps=pps, transpose_out=transpose_out),
        grid=(B, N_PAIRS // pps, nq), in_specs=in_specs, out_specs=[ospec, ospec],
        out_shape=[jax.ShapeDtypeStruct(oshape, BF16), jax.ShapeDtypeStruct(oshape, BF16)],
        scratch_shapes=[pltpu.VMEM((n_chains, tk, tq), F32), pltpu.VMEM((n_chains, tk, tq), F32)],
        compiler_params=pltpu.CompilerParams(
            dimension_semantics=("arbitrary", "arbitrary", "arbitrary"),
            vmem_limit_bytes=_vmem_limit(block_bytes, 2 * n_chains * tk * tq * 4, temp_bytes)),
        name="attn",
    )(*args)
    if not transpose_out:
        of, om = jnp.swapaxes(of, 1, 2), jnp.swapaxes(om, 1, 2)
    return of, om


def _ffn_kernel(*refs, d_ff, fc, flat_group):
    if flat_group:
        (h_ref, of_ref, om_ref, g_ref, left_ref, wof_ref, wom_ref, wout_ref, gffn_ref, wup_ref,
         cw_ref, cb_ref, wdown_ref, gfin_ref, y_ref, u_ref, l1_ref, l2_ref, _) = refs
        l1_ref[...] = jnp.zeros_like(l1_ref)
        l2_ref[...] = jnp.zeros_like(l2_ref)
        for g in range(left_ref.shape[0]):
            r0 = g * flat_group
            l1_ref[r0:r0 + 1, :] = left_ref[g, 1:2, :]
            l2_ref[r0:r0 + 1, :] = left_ref[g, 0:1, :]
            l2_ref[r0 + 1:r0 + 2, :] = left_ref[g, 1:2, :]
        ub_ref = refs[-1]
        ub_ref[0:HALO, :] = jnp.zeros((HALO, ub_ref.shape[1]), F32)
    else:
        (h_ref, of_ref, om_ref, g_ref, left_ref, wof_ref, wom_ref, wout_ref, gffn_ref, wup_ref,
         cw_ref, cb_ref, wdown_ref, gfin_ref, y_ref, u_ref, ub_ref) = refs

        @pl.when(pl.program_id(1) == 0)
        def _():
            ub_ref[HALO - (CONV_W - 1):HALO, :] = left_ref[0]

    ts, d = h_ref.shape[1], h_ref.shape[2]
    ya = jnp.dot(of_ref[0], wof_ref[...], preferred_element_type=F32)
    yb = jnp.dot(om_ref[0], wom_ref[...], preferred_element_type=F32)
    mix = g_ref[0, :, :d] * ya + g_ref[0, :, d:] * yb
    h1 = h_ref[0] + jnp.dot(mix.astype(BF16), wout_ref[...], preferred_element_type=F32)
    xn = _rms(h1, gffn_ref[...]).astype(BF16)

    if flat_group:
        row = lax.broadcasted_iota(jnp.int32, (ts, 1), 0)
        keep1 = ((row & (flat_group - 1)) >= 1).astype(F32)
        keep2 = ((row & (flat_group - 1)) >= 2).astype(F32)

    def up(c0):
        return jnp.dot(xn, wup_ref[:, c0:c0 + fc], preferred_element_type=F32)

    def conv(u, c0):
        cols = slice(c0, c0 + fc)
        ub_ref[HALO:HALO + ts, cols] = u
        u1 = ub_ref[HALO - 1:HALO - 1 + ts, cols]
        u2 = ub_ref[HALO - 2:HALO - 2 + ts, cols]
        if flat_group:
            u_ref[0, :, cols] = u
            u1 = u1 * keep1 + l1_ref[:, cols]
            u2 = u2 * keep2 + l2_ref[:, cols]
        return cb_ref[:, cols] + u2 * cw_ref[0:1, cols] + u1 * cw_ref[1:2, cols] + u * cw_ref[2:3, cols]

    n_chunks = d_ff // fc
    acc = jnp.zeros((ts, d), F32)
    ups = [(up(c * fc), up(d_ff + c * fc)) for c in range(min(UP_AHEAD, n_chunks))]
    for c in range(n_chunks):
        u_gate, u_val = ups.pop(0)
        if c + UP_AHEAD < n_chunks:
            ups.append((up((c + UP_AHEAD) * fc), up(d_ff + (c + UP_AHEAD) * fc)))
        act = (jax.nn.silu(conv(u_gate, c * fc)) * conv(u_val, d_ff + c * fc)).astype(BF16)
        acc = acc + jnp.dot(act, wdown_ref[c * fc:(c + 1) * fc, :], preferred_element_type=F32)
    y_ref[0] = _rms(h1 + acc, gfin_ref[...])
    if not flat_group:
        tail = ub_ref[HALO + ts - (CONV_W - 1):HALO + ts, :]
        u_ref[0] = tail
        ub_ref[HALO - (CONV_W - 1):HALO, :] = tail


def _merge_ffn(h, of, om, gates, left, wts, *, ts, flat_group=0):
    B, n, d = h.shape
    d_ff = wts['w_down'].shape[0]
    fc = MXU_DIM
    assert n % ts == 0 and d_ff % fc == 0
    row = lambda w: pl.BlockSpec((1, ts, w), lambda b, t: (b, t, 0))
    consts = [wts[k] for k in ('w_o_fox', 'w_o_mla', 'w_out', 'g_ffn', 'w_up', 'conv_w', 'conv_b', 'w_down', 'g_fin')]
    const_specs = [_const_spec(c.shape) for c in consts]
    in_specs = [row(d), row(of.shape[2]), row(om.shape[2]), row(2 * d)]
    args = [h, of, om, gates, left]
    if flat_group:
        assert B == 1 and ts % flat_group == 0 and left.shape[0] * flat_group == n
        gpt = ts // flat_group
        in_specs.append(pl.BlockSpec((gpt, CONV_W - 1, 2 * d_ff), lambda b, t: (t, 0, 0)))
        u_shape, u_spec = (1, n, 2 * d_ff), row(2 * d_ff)
        scratch = [pltpu.VMEM((ts, 2 * d_ff), F32), pltpu.VMEM((ts, 2 * d_ff), F32)]
        const_extra = 2 * ts * 2 * d_ff * 4 + 2 * gpt * 8 * 2 * d_ff * 4
    else:
        bl = left.shape[0]
        in_specs.append(pl.BlockSpec((1, CONV_W - 1, 2 * d_ff), lambda b, t: (b if bl > 1 else 0, 0, 0)))
        u_shape = (B, CONV_W - 1, 2 * d_ff)
        u_spec = pl.BlockSpec((1, CONV_W - 1, 2 * d_ff), lambda b, t: (b, 0, 0))
        scratch = []
        const_extra = 0
    scratch.append(pltpu.VMEM((HALO + ts, 2 * d_ff), F32))
    const_extra += (HALO + ts) * 2 * d_ff * 4
    u_rows = ts if flat_group else 8
    block_bytes = ts * (d * 4 + (of.shape[2] + om.shape[2]) * 2 + 2 * d * 4 + d * 4) + u_rows * 2 * d_ff * 4
    const_bytes = sum(c.size * c.dtype.itemsize for c in consts) + const_extra
    temp_bytes = ts * d * 4 * 8 + ts * fc * 4 * 16
    y, u = pl.pallas_call(
        functools.partial(_ffn_kernel, d_ff=d_ff, fc=fc, flat_group=flat_group),
        grid=(B, n // ts), in_specs=in_specs + const_specs,
        out_specs=[row(d), u_spec],
        out_shape=[jax.ShapeDtypeStruct((B, n, d), F32), jax.ShapeDtypeStruct(u_shape, F32)],
        scratch_shapes=scratch,
        compiler_params=pltpu.CompilerParams(
            dimension_semantics=("arbitrary", "arbitrary"),
            vmem_limit_bytes=_vmem_limit(block_bytes, const_bytes, temp_bytes)),
        name="ffn",
    )(*args, *consts)
    return y, u


def _rope_tables(pos):
    half = MLA_ROPE // 2
    inv = ROPE_BASE ** (-jnp.arange(0, MLA_ROPE, 2, dtype=F32) / MLA_ROPE)
    ang = pos.astype(F32)[:, None] * inv[None, :]
    cos, sin = jnp.cos(ang), jnp.sin(ang)
    zero = jnp.zeros_like(sin)
    reps = LANES // MLA_ROPE
    ct = jnp.tile(jnp.concatenate([cos, cos], axis=1), (1, reps))
    sa = jnp.tile(jnp.concatenate([-sin, zero], axis=1), (1, reps))
    sb = jnp.tile(jnp.concatenate([zero, sin], axis=1), (1, reps))
    assert half * 2 == MLA_ROPE
    return ct, sa, sb


def _lower_tri(n, group):
    i = np.arange(n)
    m = (i[None, :] <= i[:, None]) & ((i[None, :] // group) == (i[:, None] // group))
    return jnp.asarray(m, BF16)


def _prepare_weights(norm_mix_g, w_in, b_forget, mla_q_norm_g, w_uq, mla_kv_norm_g, w_ukv, w_o_fox, w_o_mla, w_out,
                     norm_ffn_g, w_up, conv_w, conv_b, w_down, norm_final_g):
    d = w_in.shape[0]
    q_lora, kv_lora = mla_q_norm_g.shape[0], mla_kv_norm_g.shape[0]
    off_ff = 3 * FOX_W
    off_cq = off_ff + FOX_HEADS
    off_ckv = off_cq + q_lora
    off_kr = off_ckv + kv_lora
    off_gate = off_kr + MLA_ROPE
    pad = LANES - F_PARTS * FOX_HEADS
    w_ff = jnp.pad(jnp.tile(w_in[:, off_ff:off_cq], (1, F_PARTS)), ((0, 0), (0, pad)))
    w_a = jnp.concatenate([
        w_in[:, :off_ff], w_ff, w_in[:, off_cq:off_kr],
        jnp.tile(w_in[:, off_kr:off_gate], (1, LANES // MLA_ROPE)),
        w_in[:, off_gate:]], axis=1).astype(BF16)
    uq = w_uq.reshape(q_lora, MLA_HEADS, MLA_QK)
    w_uq_p = jnp.concatenate([uq[:, :, :MLA_NOPE].reshape(q_lora, -1), uq[:, :, MLA_NOPE:].reshape(q_lora, -1)], axis=1)
    ukv = w_ukv.reshape(kv_lora, MLA_HEADS, MLA_NOPE + MLA_V)
    w_ukv_p = jnp.concatenate([ukv[:, :, :MLA_NOPE].reshape(kv_lora, -1), ukv[:, :, MLA_NOPE:].reshape(kv_lora, -1)], axis=1)
    return dict(
        g_mix=norm_mix_g.reshape(1, d), w_a=w_a,
        b_exp=jnp.pad(jnp.tile(b_forget, F_PARTS), (0, pad)).reshape(1, LANES),
        g_q=mla_q_norm_g.reshape(1, q_lora), w_uq=w_uq_p.astype(BF16),
        g_kv=mla_kv_norm_g.reshape(1, kv_lora), w_ukv=w_ukv_p.astype(BF16),
        w_o_fox=w_o_fox.astype(BF16), w_o_mla=w_o_mla.astype(BF16), w_out=w_out.astype(BF16),
        g_ffn=norm_ffn_g.reshape(1, d), w_up=w_up.astype(BF16), conv_w=conv_w, conv_b=conv_b.reshape(1, -1),
        w_down=w_down.astype(BF16), g_fin=norm_final_g.reshape(1, d))


PROJ_ROWS = 512
FFN_ROWS = 256
PROMPT_TQ = 256
PROMPT_PAIRS = 4
SMALL_ROWS = LANES
SMALL_PAIRS = 2


def kernel(x_prompt, x_sample, cache_fox_k, cache_fox_v, cache_fox_logf, cache_mla_ckv, cache_mla_krope, state_ffn_conv, meta_tokens, norm_mix_g, w_in, b_forget, mla_q_norm_g, w_uq, mla_kv_norm_g, w_ukv, w_o_fox, w_o_mla, w_out, norm_ffn_g, w_up, conv_w, conv_b, w_down, norm_final_g):
    assert w_in.shape[0] == 1, "single-layer model"
    B, S, d = x_prompt.shape
    Bs, ns, _ = x_sample.shape
    P = cache_fox_k.shape[2]
    n_meta = meta_tokens.shape[0]
    assert n_meta == N_META == ns
    wts = _prepare_weights(norm_mix_g[0], w_in[0], b_forget[0], mla_q_norm_g[0], w_uq[0], mla_kv_norm_g[0], w_ukv[0],
                           w_o_fox[0], w_o_mla[0], w_out[0], norm_ffn_g[0], w_up[0], conv_w[0], conv_b[0], w_down[0],
                           norm_final_g)
    d_ff2 = w_up.shape[2]

    r_pad = -(-(Bs + 1) * ns // SMALL_ROWS) * SMALL_ROWS
    n_groups = r_pad // ns
    n_fill = r_pad - (Bs + 1) * ns
    rows = jnp.concatenate([x_sample.reshape(Bs * ns, d), jnp.zeros((n_fill, d), x_sample.dtype),
                            meta_tokens.astype(x_sample.dtype)], axis=0)[None]
    pos_small = jnp.concatenate([jnp.tile(P + jnp.arange(ns), Bs), jnp.zeros((n_fill,), jnp.int32), jnp.arange(n_meta)])
    zero_f = jnp.zeros((1, 8, LANES), F32)
    sm = _project(rows, _rope_tables(pos_small), zero_f, wts, _lower_tri(r_pad, ns), ts=r_pad, cs=r_pad)
    sm_b = {k: (jnp.swapaxes(v.reshape(v.shape[1], n_groups, ns), 0, 1) if k in TRANSPOSED
                else v.reshape(n_groups, ns, v.shape[-1])) for k, v in sm.items() if k != 'ftot'}
    live = lambda a: jnp.concatenate([a[:Bs], a[n_groups - 1:]], axis=0)

    feat_major = lambda a: jnp.moveaxis(a[0], 1, -1).reshape(Bs, -1, P)
    prefix_s = _prefix_operands(
        feat_major(cache_fox_k), feat_major(cache_fox_v), feat_major(cache_fox_logf), cache_mla_ckv[0],
        feat_major(cache_mla_krope), wts['w_ukv'], _lower_tri(MXU_DIM, MXU_DIM), cs=MXU_DIM)
    of_s, om_s = _attention({k: live(v) for k, v in sm_b.items()}, prefix_s, tq=ns, tk=ns, tkp=P, pps=SMALL_PAIRS, n_pref=Bs)

    def spread(a):
        return jnp.concatenate([a[:Bs].reshape(Bs * ns, -1), jnp.zeros((n_fill, a.shape[-1]), a.dtype), a[Bs]], axis=0)[None]

    left_groups = jnp.concatenate([state_ffn_conv[0], jnp.zeros((n_groups - Bs, CONV_W - 1, d_ff2), F32)], axis=0)
    y_s, u_s = _merge_ffn(rows, spread(of_s), spread(om_s), sm['gate'], left_groups, wts, ts=SMALL_ROWS, flat_group=ns)
    y_sample = y_s[0, :Bs * ns].reshape(Bs, ns, d)
    u_groups = u_s.reshape(n_groups, ns, d_ff2)
    conv_s = u_groups[:Bs, ns - (CONV_W - 1):]
    left_meta = u_groups[n_groups - 1:, ns - (CONV_W - 1):]

    meta_ops = {k: sm_b[k][n_groups - 1:] for k in ('kfa', 'vf', 'kc', 'vm')}
    pos_p = n_meta + jnp.arange(S)
    pr = _project(x_prompt, _rope_tables(pos_p), sm['ftot'], wts, _lower_tri(MXU_DIM, MXU_DIM), ts=PROJ_ROWS,
                  cs=MXU_DIM, head_rows={k: sm_b[k][n_groups - 1] for k in CACHE_KEYS})
    of_p, om_p = _attention(pr, meta_ops, tq=PROMPT_TQ, tk=PROMPT_TQ, tkp=n_meta, pps=PROMPT_PAIRS)
    y_prompt, conv_p = _merge_ffn(x_prompt, of_p, om_p, pr['gate'], left_meta, wts, ts=FFN_ROWS)
    caches_p = [pr[k].reshape((1, B, n_meta + S) + ((FOX_HEADS, FOX_HEAD_DIM) if k in ('fk', 'fv') else pr[k].shape[3:]))
                for k in CACHE_KEYS]

    def sample_rows(name, tail):
        return sm_b[name][:Bs].reshape((1, Bs, ns) + tail)

    hd = (FOX_HEADS, FOX_HEAD_DIM)
    return (y_prompt, y_sample, *caches_p, conv_p[None],
            sample_rows('fk', hd), sample_rows('fv', hd), sample_rows('logf', (FOX_HEADS,)),
            sample_rows('ckv', (cache_mla_ckv.shape[-1],)), sample_rows('kr', (MLA_ROPE,)), conv_s[None])
```

```python
import functools
import math

import jax
import jax.numpy as jnp
import numpy as np
from jax import lax
from jax.experimental import pallas as pl
from jax.experimental.pallas import tpu as pltpu

F32 = jnp.float32
BF16 = jnp.bfloat16

N_META = 16
CHUNK = 64
EPS = 1e-6
NEG = -1e30
FOX_HEADS = 8
FOX_HEAD_DIM = 64
MLA_HEADS = 8
MLA_NOPE = 64
MLA_ROPE = 32
MLA_V = 64
MLA_QK = MLA_NOPE + MLA_ROPE
ROPE_BASE = 10000.0
CONV_W = 3

FOX_W = FOX_HEADS * FOX_HEAD_DIM
LOG2E = math.log2(math.e)
FOX_QSCALE = LOG2E / math.sqrt(FOX_HEAD_DIM)
MLA_QSCALE = LOG2E / math.sqrt(MLA_QK)

LANES = 128
MXU_DIM = 256
VMEM_BYTES_V7X = 64 * 1024 * 1024
F_PARTS = 3
HEADS_PER_PAIR = LANES // FOX_HEAD_DIM
N_PAIRS = FOX_HEADS // HEADS_PER_PAIR
PAIR_W = 2 * LANES
HALO = 8
GATE_CHUNKS = 4
UP_AHEAD = 3

C_FQ = 0
C_FK = C_FQ + FOX_W
C_FV = C_FK + FOX_W
C_FF = C_FV + FOX_W
C_CQ = C_FF + LANES


def _rms(x, g):
    return x * lax.rsqrt(jnp.mean(x * x, axis=-1, keepdims=True) + EPS) * g


def _split3(x):
    hi = x.astype(BF16)
    r = x - hi.astype(F32)
    mid = r.astype(BF16)
    lo = (r - mid.astype(F32)).astype(BF16)
    return hi, mid, lo


def _cumsum_rows(lf, tri, carry, cs):
    hi, mid, lo = _split3(lf)
    pieces = []
    for c in range(lf.shape[0] // cs):
        sl = slice(c * cs, (c + 1) * cs)
        fc = (jnp.dot(tri, hi[sl], preferred_element_type=F32)
              + jnp.dot(tri, mid[sl], preferred_element_type=F32)
              + jnp.dot(tri, lo[sl], preferred_element_type=F32)) + carry
        carry = fc[cs - 1:cs, :]
        pieces.append(fc)
    f = pieces[0] if len(pieces) == 1 else jnp.concatenate(pieces, axis=0)
    return f, carry


def _forget_columns(f):
    hi, mid, lo = _split3(f)
    lane = lax.broadcasted_iota(jnp.int32, f.shape, 1)
    zero = jnp.zeros_like(hi)
    return jnp.where(lane < FOX_HEADS, hi,
                     jnp.where(lane < 2 * FOX_HEADS, mid,
                               jnp.where(lane < F_PARTS * FOX_HEADS, lo, zero)))


def _store_pairs(ref, main, extra):
    for j in range(N_PAIRS):
        ref[0, :, PAIR_W * j:PAIR_W * j + LANES] = main[:, LANES * j:LANES * (j + 1)]
        ref[0, :, PAIR_W * j + LANES:PAIR_W * (j + 1)] = extra


def _store_cache(ref, val, prev_ref=None):
    lead = (0,) * (len(ref.shape) - 2)
    width = prev_ref.shape[1] if val is None else val.shape[1]
    heads = width // ref.shape[-1]

    def put(row0, v):
        rows = v.shape[0]
        if heads == 1:
            ref[lead + (slice(row0, row0 + rows), slice(None))] = v
        else:
            dh = ref.shape[-1]
            for h in range(heads):
                ref[lead + (pl.ds(row0 * heads + h, rows, stride=heads), slice(None))] = v[:, h * dh:(h + 1) * dh]

    if prev_ref is None:
        assert ref.shape[-2] == val.shape[0] * heads
        put(0, val)
        return
    r0 = prev_ref.shape[0]
    put(0, prev_ref[...])
    if val is not None:
        assert ref.shape[-2] == val.shape[0] * heads
        put(r0, val[:val.shape[0] - r0, :])
        prev_ref[...] = val[val.shape[0] - r0:, :]


def _rope(x, ct, sa, sb):
    return x * ct + pltpu.roll(x, LANES - MLA_ROPE // 2, 1) * sa + pltpu.roll(x, MLA_ROPE // 2, 1) * sb


def _proj_kernel(*refs, nt, with_head_rows, **kw):
    n_in = 13 + (len(CACHE_KEYS) if with_head_rows else 0)
    cache_refs = refs[n_in:n_in + len(CACHE_KEYS)]
    carry_ref = refs[n_in + 14]
    prev = dict(zip(CACHE_KEYS, refs[n_in + 15:])) if with_head_rows else dict.fromkeys(CACHE_KEYS)
    t = pl.program_id(1)

    @pl.when(t == 0)
    def _():
        carry_ref[...] = refs[4][0]
        if with_head_rows:
            for k, head_ref in zip(CACHE_KEYS, refs[13:n_in]):
                prev[k][...] = head_ref[...]

    tile_refs = (refs[:13], refs[n_in:n_in + 15], prev)
    if not with_head_rows:
        _proj_tile(*tile_refs, **kw)
        return

    @pl.when(t < nt - 1)
    def _():
        _proj_tile(*tile_refs, **kw)

    @pl.when(t == nt - 1)
    def _():
        _proj_tile(*tile_refs, gates=False, **kw)

    @pl.when(t == nt)
    def _():
        _proj_tile(*tile_refs, main=False, **kw)
        for k, ref in zip(CACHE_KEYS, cache_refs):
            _store_cache(ref, None, prev[k])


def _proj_tile(in_refs, out_refs, prev, *, cs, q_lora, kv_lora, d_model, main=True, gates=True):
    (x_ref, ct_ref, sa_ref, sb_ref, finit_ref, gmix_ref, w_ref, bexp_ref, gq_ref, wuq_ref,
     gkv_ref, wukv_ref, tri_ref) = in_refs
    (fk_ref, fv_ref, logf_ref, ckv_ref, kr_ref, qf_ref, kfa_ref, vf_ref, qn_ref, qr_ref,
     kc_ref, vm_ref, gate_ref, ftot_ref, carry_ref) = out_refs
    c_ckv = C_CQ + q_lora
    c_kr = c_ckv + kv_lora
    c_gate = c_kr + LANES
    xb = _rms(x_ref[0], gmix_ref[...]).astype(BF16)

    def proj(a, b):
        return jnp.dot(xb, w_ref[:, a:b], preferred_element_type=F32)

    def merge_gates():
        gw = 2 * d_model // GATE_CHUNKS
        for c in range(GATE_CHUNKS):
            gate_ref[0, :, c * gw:(c + 1) * gw] = jax.nn.sigmoid(proj(c_gate + c * gw, c_gate + (c + 1) * gw))

    if not main:
        merge_gates()
        return

    ct, sa, sb = ct_ref[...], sa_ref[...], sb_ref[...]
    nope_w = MLA_HEADS * MLA_NOPE
    cq = proj(C_CQ, c_ckv)
    ckv = proj(c_ckv, c_kr)
    ff = proj(C_FF, C_CQ)
    kr = proj(c_kr, c_gate)

    qf_ref[0] = (proj(C_FQ, C_FK) * FOX_QSCALE).T.astype(BF16)
    fk = proj(C_FK, C_FV)
    _store_cache(fk_ref, fk, prev['fk'])
    fv = proj(C_FV, C_FF)
    _store_cache(fv_ref, fv, prev['fv'])
    vf_ref[0] = fv.T.astype(BF16)

    cqn = _rms(cq, gq_ref[...]).astype(BF16)
    ckvn = _rms(ckv, gkv_ref[...])
    _store_cache(ckv_ref, ckvn, prev['ckv'])
    q = jnp.dot(cqn, wuq_ref[...], preferred_element_type=F32) * MLA_QSCALE
    kv = jnp.dot(ckvn.astype(BF16), wukv_ref[...], preferred_element_type=F32)

    lf = jax.nn.log_sigmoid(ff + bexp_ref[...])
    _store_cache(logf_ref, lf[:, :FOX_HEADS], prev['logf'])
    f, carry = _cumsum_rows(lf, tri_ref[...], carry_ref[0:1, :], cs)
    carry_ref[0:1, :] = carry
    ftot_ref[0] = jnp.broadcast_to(carry, ftot_ref.shape[1:])

    if gates:
        merge_gates()

    _store_pairs(kfa_ref, fk.astype(BF16), _forget_columns(f * LOG2E))
    qn_ref[0] = q[:, :nope_w].T.astype(BF16)
    for c in range(MLA_HEADS * MLA_ROPE // LANES):
        qr_ref[0, LANES * c:LANES * (c + 1), :] = _rope(
            q[:, nope_w + LANES * c:nope_w + LANES * (c + 1)], ct, sa, sb).T.astype(BF16)
    vm_ref[0] = kv[:, nope_w:].T.astype(BF16)
    kr4 = _rope(kr, ct, sa, sb)
    _store_cache(kr_ref, kr4[:, :MLA_ROPE], prev['kr'])
    _store_pairs(kc_ref, kv[:, :nope_w].astype(BF16), kr4.astype(BF16))


TRANSPOSED = ('qf', 'vf', 'qn', 'qr', 'vm')
CACHE_KEYS = ('fk', 'fv', 'logf', 'ckv', 'kr')


def _const_spec(shape):
    return pl.BlockSpec(shape, lambda *_: (0,) * len(shape), pipeline_mode=pl.Buffered(1))


def _vmem_limit(block_bytes, const_bytes, temp_bytes):
    need = 2 * block_bytes + const_bytes + temp_bytes
    assert need < VMEM_BYTES_V7X, need
    return int(need)


def _project(x, tables, finit, wts, tri, *, ts, cs, head_rows=None):
    B, n, d = x.shape
    assert n % ts == 0 and ts % cs == 0
    nt = n // ts
    q_lora = wts['g_q'].shape[1]
    kv_lora = wts['g_kv'].shape[1]
    ct, sa, sb = tables
    tile = (lambda t: jnp.minimum(t, nt - 1)) if head_rows is not None else (lambda t: t)
    row = lambda w: pl.BlockSpec((1, ts, w), lambda b, t: (b, tile(t), 0))
    tab = pl.BlockSpec((ts, LANES), lambda b, t: (tile(t), 0))
    in_specs = [row(d), tab, tab, tab,
                pl.BlockSpec((1, 8, LANES), lambda b, t: (0, 0, 0)),
                _const_spec(wts['g_mix'].shape), _const_spec(wts['w_a'].shape), _const_spec(wts['b_exp'].shape),
                _const_spec(wts['g_q'].shape), _const_spec(wts['w_uq'].shape), _const_spec(wts['g_kv'].shape),
                _const_spec(wts['w_ukv'].shape), _const_spec(tri.shape)]
    widths = dict(fk=(FOX_W, F32), fv=(FOX_W, F32), logf=(FOX_HEADS, F32), ckv=(kv_lora, F32), kr=(MLA_ROPE, F32),
                  qf=(FOX_W, BF16), kfa=(N_PAIRS * PAIR_W, BF16), vf=(FOX_W, BF16),
                  qn=(MLA_HEADS * MLA_NOPE, BF16), qr=(MLA_HEADS * MLA_ROPE, BF16),
                  kc=(N_PAIRS * PAIR_W, BF16), vm=(MLA_HEADS * MLA_V, BF16), gate=(2 * d, F32))
    col = lambda w: pl.BlockSpec((1, w, ts), lambda b, t: (b, 0, tile(t)))
    out_shape = [jax.ShapeDtypeStruct((B, w, n) if k in TRANSPOSED else (B, n, w), dt) for k, (w, dt) in widths.items()]
    out_specs = [col(w) if k in TRANSPOSED else row(w) for k, (w, _) in widths.items()]
    args = [x, ct, sa, sb, finit, wts['g_mix'], wts['w_a'], wts['b_exp'], wts['g_q'], wts['w_uq'], wts['g_kv'],
            wts['w_ukv'], tri]
    scratch = [pltpu.VMEM((8, LANES), F32)]
    if head_rows is not None:
        r0 = head_rows[CACHE_KEYS[0]].shape[0]
        assert r0 % 8 == 0 and ts % 8 == 0 and r0 < ts
        for idx, k in enumerate(CACHE_KEYS):
            heads, w = (FOX_HEADS, FOX_HEAD_DIM) if k in ('fk', 'fv') else (1, widths[k][0])
            out_shape[idx] = jax.ShapeDtypeStruct((1, B, (r0 + n) * heads, w), F32)
            out_specs[idx] = pl.BlockSpec((1, 1, ts * heads, w), lambda b, t: (0, b, t, 0))
            args.append(head_rows[k])
            in_specs.append(_const_spec(head_rows[k].shape))
            scratch.append(pltpu.VMEM(head_rows[k].shape, F32))
    out_shape.append(jax.ShapeDtypeStruct((B, 8, LANES), F32))
    out_specs.append(pl.BlockSpec((1, 8, LANES), lambda b, t: (b, 0, 0)))
    block_bytes = ts * (d * 4 + 3 * LANES * 4 + sum(w * jnp.dtype(dt).itemsize for w, dt in widths.values()))
    if head_rows is not None:
        block_bytes += 2 * ts * FOX_HEADS * LANES * 4
    const_bytes = sum(int(np.prod(wts[k].shape)) * wts[k].dtype.itemsize for k in ('w_a', 'w_uq', 'w_ukv')) + tri.size * 2
    temp_bytes = ts * (2 * d + 2 * d) * 4 * 2
    outs = pl.pallas_call(
        functools.partial(_proj_kernel, cs=cs, q_lora=q_lora, kv_lora=kv_lora, d_model=d, nt=nt,
                          with_head_rows=head_rows is not None),
        grid=(B, nt + (head_rows is not None)), in_specs=in_specs, out_specs=out_specs, out_shape=out_shape,
        scratch_shapes=scratch,
        compiler_params=pltpu.CompilerParams(
            dimension_semantics=("arbitrary", "arbitrary"),
            vmem_limit_bytes=_vmem_limit(block_bytes, const_bytes, temp_bytes)),
        name="proj",
    )(*args)
    res = dict(zip(widths.keys(), outs[:-1]))
    res['ftot'] = outs[-1]
    return res


def _prefix_kernel(ckt_ref, cvt_ref, lft_ref, ckv_ref, krt_ref, wukv_ref, tri_ref,
                   kfa_ref, vf_ref, kc_ref, vm_ref, *, cs):
    P = ckt_ref.shape[2]
    lft = lft_ref[0]
    lf_exp = jnp.concatenate([lft] * F_PARTS + [jnp.zeros((LANES - F_PARTS * FOX_HEADS, P), F32)], axis=0).T
    f, total = _cumsum_rows(lf_exp, tri_ref[...], jnp.zeros((1, LANES), F32), cs)
    _store_pairs(kfa_ref, ckt_ref[0].T.astype(BF16), _forget_columns((f - total) * LOG2E))
    vf_ref[0] = cvt_ref[0].astype(BF16)
    kv = jnp.dot(ckv_ref[0].astype(BF16), wukv_ref[...], preferred_element_type=F32)
    nope_w = MLA_HEADS * MLA_NOPE
    vm_ref[0] = kv[:, nope_w:].T.astype(BF16)
    kr4 = jnp.concatenate([krt_ref[0]] * (LANES // MLA_ROPE), axis=0).T
    _store_pairs(kc_ref, kv[:, :nope_w].astype(BF16), kr4.astype(BF16))


def _prefix_operands(ckt, cvt, lft, cckv, krt, w_ukv, tri, *, cs):
    B, _, P = ckt.shape
    row = lambda r, w: pl.BlockSpec((1, r, w), lambda b: (b, 0, 0))
    widths = dict(kfa=N_PAIRS * PAIR_W, vf=FOX_W, kc=N_PAIRS * PAIR_W, vm=MLA_HEADS * MLA_V)
    ins = (ckt, cvt, lft, cckv, krt)
    block_bytes = sum(max(a.shape[1], 8) * max(a.shape[2], LANES) * 4 for a in ins) + P * sum(widths.values()) * 2
    outs = pl.pallas_call(
        functools.partial(_prefix_kernel, cs=cs),
        grid=(B,),
        in_specs=[row(a.shape[1], a.shape[2]) for a in ins] + [_const_spec(w_ukv.shape), _const_spec(tri.shape)],
        out_specs=[row(w, P) if k in TRANSPOSED else row(P, w) for k, w in widths.items()],
        out_shape=[jax.ShapeDtypeStruct((B, w, P) if k in TRANSPOSED else (B, P, w), BF16) for k, w in widths.items()],
        compiler_params=pltpu.CompilerParams(
            dimension_semantics=("arbitrary",),
            vmem_limit_bytes=_vmem_limit(block_bytes, w_ukv.size * 2 + tri.size * 2, P * 1024 * 4 * 4)),
        name="prefix",
    )(*ins, w_ukv, tri)
    return dict(zip(widths.keys(), outs))


def _scores(ks, qts):
    return tuple(jnp.dot(k, qt, preferred_element_type=F32) for k, qt in zip(ks, qts))


def _online_update(states, scores, vts, masks=None, extra=None):
    mids = []
    for c, ((m, l, acc), s) in enumerate(zip(states, scores)):
        if masks is not None:
            s = jnp.where(masks[c], s, NEG)
        n_extra = 0
        if extra is not None:
            n_extra = extra[0][c].shape[0]
            s = jnp.concatenate([extra[0][c], s], axis=0)
        m_new = jnp.maximum(m, jnp.max(s, axis=0, keepdims=True))
        alpha = jnp.exp2(m - m_new)
        p = jnp.exp2(s - m_new)
        mids.append((m_new, alpha * l + jnp.sum(p, axis=0, keepdims=True), alpha * acc, p.astype(BF16), n_extra))
    out = []
    for c, ((m_new, l, acc, p, n_extra), vt) in enumerate(zip(mids, vts)):
        if n_extra:
            acc = acc + jnp.dot(extra[1][c], p[:n_extra], preferred_element_type=F32)
        out.append((m_new, l, acc + jnp.dot(vt, p[n_extra:], preferred_element_type=F32)))
    return tuple(out)


def _attn_kernel(*refs, tq, tk, tkp, nq, pps, has_prefix, n_pref, transpose_out):
    if has_prefix:
        (qf_ref, qn_ref, qr_ref, kfa_ref, vf_ref, kc_ref, vm_ref,
         pkfa_ref, pvf_ref, pkc_ref, pvm_ref, of_ref, om_ref, sa_ref, sb_ref) = refs
    else:
        qf_ref, qn_ref, qr_ref, kfa_ref, vf_ref, kc_ref, vm_ref, of_ref, om_ref, sa_ref, sb_ref = refs
        pkfa_ref = pvf_ref = pkc_ref = pvm_ref = None
    b, j, i = pl.program_id(0), pl.program_id(1), pl.program_id(2)
    pref_bias = None if (not has_prefix or n_pref is None) else jnp.where(b < n_pref, 0.0, NEG)

    frow = lax.broadcasted_iota(jnp.int32, (LANES, tq), 0)
    zero_half = jnp.zeros((FOX_HEAD_DIM, tq), BF16)
    rope_groups = LANES // MLA_ROPE
    rope_rows_per_step = pps * HEADS_PER_PAIR * MLA_ROPE
    chains = []
    for mixer in range(2):
        for p in range(pps):
            for s in range(HEADS_PER_PAIR):
                h = HEADS_PER_PAIR * (pps * j + p) + s
                vrows = slice(p * LANES + s * FOX_HEAD_DIM, p * LANES + (s + 1) * FOX_HEAD_DIM)
                own = [zero_half] * HEADS_PER_PAIR
                own[s] = (qf_ref if mixer == 0 else qn_ref)[0, vrows, :]
                if mixer == 0:
                    extra = jnp.where((frow < F_PARTS * FOX_HEADS) & ((frow & (FOX_HEADS - 1)) == h),
                                      -1.0, 0.0).astype(BF16)
                else:
                    r0 = (p * HEADS_PER_PAIR * MLA_ROPE // LANES) * LANES if rope_rows_per_step > LANES else 0
                    qr = qr_ref[0, r0:r0 + LANES, :]
                    extra = jnp.where((frow >> int(math.log2(MLA_ROPE))) == (h & (rope_groups - 1)), qr, jnp.zeros_like(qr))
                chains.append((jnp.concatenate(own + [extra], axis=0), slice(p * PAIR_W, (p + 1) * PAIR_W), vrows, mixer))
    qts = [c[0] for c in chains]
    key_refs, val_refs = (kfa_ref, kc_ref), (vf_ref, vm_ref)
    pkey_refs, pval_refs = (pkfa_ref, pkc_ref), (pvf_ref, pvm_ref)

    def scores(krefs, rows):
        return _scores([krefs[c[3]][0, rows, c[1]] for c in chains], qts)

    def update(sts, sc, vrefs, rows, **kw):
        return _online_update(sts, sc, [vrefs[c[3]][0, c[2], rows] for c in chains], **kw)

    assert tq == tk
    tile_rows = (lambda t: pl.ds(pl.multiple_of(t * tk, tk), tk)) if nq > 1 else (lambda t: slice(0, tk))

    def put_scores(ref, t):
        for c, s in enumerate(scores(key_refs, tile_rows(t))):
            ref[c] = s

    def step(sts, ref, t, **kw):
        return update(sts, [ref[c] for c in range(len(chains))], val_refs, tile_rows(t), **kw)

    prefix = None
    if has_prefix:
        assert tkp == pkfa_ref.shape[1]
        all_rows = slice(0, tkp)
        psc = scores(pkey_refs, all_rows)
        if pref_bias is not None:
            psc = [s + pref_bias for s in psc]
        prefix = (psc, [pval_refs[c[3]][0, c[2], all_rows] for c in chains])
    put_scores(sa_ref, 0)

    def finish(sts, ref):
        krow = lax.broadcasted_iota(jnp.int32, (tq, tq), 0)
        qcol = lax.broadcasted_iota(jnp.int32, (tq, tq), 1)
        chunk_shift = int(math.log2(CHUNK))
        masks = (krow <= qcol, (krow >> chunk_shift) <= (qcol >> chunk_shift))
        sts = step(sts, ref, i, masks=[masks[c[3]] for c in chains], extra=prefix)
        outs = [acc / l for _, l, acc in sts]
        half = len(outs) // 2
        for o_ref, group in ((of_ref, outs[:half]), (om_ref, outs[half:])):
            ot = jnp.concatenate(group, axis=0)
            o_ref[0] = (ot.T if transpose_out else ot).astype(BF16)

    init = (jnp.full((1, tq), NEG, F32), jnp.zeros((1, tq), F32), jnp.zeros((FOX_HEAD_DIM, tq), F32))
    states = tuple(init for _ in chains)
    if nq == 1:
        finish(states, sa_ref)
    else:
        def body(u, sts):
            put_scores(sb_ref, 2 * u + 1)
            sts = step(sts, sa_ref, 2 * u)
            put_scores(sa_ref, 2 * u + 2)
            return step(sts, sb_ref, 2 * u + 1)

        states = lax.fori_loop(0, i >> 1, body, states)

        @pl.when((i & 1) == 0)
        def _():
            finish(states, sa_ref)

        @pl.when((i & 1) == 1)
        def _():
            put_scores(sb_ref, i)
            finish(step(states, sa_ref, i - 1), sb_ref)


def _attention(ops, prefix, *, tq, tk, tkp, pps, n_pref=None):
    B, n, _ = ops['kfa'].shape
    assert n % tq == 0 and tq % tk == 0 and N_PAIRS % pps == 0
    nq = n // tq
    transpose_out = tq % LANES == 0
    rope_w = max(LANES, pps * HEADS_PER_PAIR * MLA_ROPE)
    steps_per_rope_block = rope_w // (pps * HEADS_PER_PAIR * MLA_ROPE)
    qspec = pl.BlockSpec((1, pps * LANES, tq), lambda b, j, i: (b, j, i))
    qrspec = pl.BlockSpec((1, rope_w, tq), lambda b, j, i: (b, j // steps_per_rope_block, i))
    kspec = pl.BlockSpec((1, n, pps * PAIR_W), lambda b, j, i: (b, 0, j))
    vspec = pl.BlockSpec((1, pps * LANES, n), lambda b, j, i: (b, j, 0))
    args = [ops['qf'], ops['qn'], ops['qr'], ops['kfa'], ops['vf'], ops['kc'], ops['vm']]
    in_specs = [qspec, qspec, qrspec, kspec, vspec, kspec, vspec]
    block_bytes = 2 * pps * (3 * tq * LANES + 2 * n * (PAIR_W + LANES) + 2 * tq * LANES)
    if prefix is not None:
        Bp, P, _ = prefix['kfa'].shape
        assert P % tkp == 0
        if Bp == 1:
            pb = lambda b: 0
        elif n_pref is not None:
            pb = lambda b: jnp.minimum(b, n_pref - 1)
        else:
            pb = lambda b: b
        pk = pl.BlockSpec((1, P, pps * PAIR_W), lambda b, j, i: (pb(b), 0, j))
        pv = pl.BlockSpec((1, pps * LANES, P), lambda b, j, i: (pb(b), j, 0))
        args += [prefix['kfa'], prefix['vf'], prefix['kc'], prefix['vm']]
        in_specs += [pk, pv, pk, pv]
        block_bytes += 2 * 2 * pps * P * (PAIR_W + LANES)
    if transpose_out:
        ospec = pl.BlockSpec((1, tq, pps * LANES), lambda b, j, i: (b, i, j))
        oshape = (B, n, N_PAIRS * LANES)
    else:
        ospec = pl.BlockSpec((1, pps * LANES, tq), lambda b, j, i: (b, j, i))
        oshape = (B, N_PAIRS * LANES, n)
    n_chains = 2 * HEADS_PER_PAIR * pps
    tq_pad = max(tq, LANES)
    temp_bytes = n_chains * 3 * tq_pad * max(tk, tkp if prefix is not None else tk) * 4 + 4 * n_chains * tq_pad * LANES * 4
    of, om = pl.pallas_call(
        functools.partial(_attn_kernel, tq=tq, tk=tk, tkp=tkp, has_prefix=prefix is not None, n_pref=n_pref,
                          nq=nq, pps=pps, transpose_out=transpose_out),
        grid=(B, N_PAIRS // pps, nq), in_specs=in_specs, out_specs=[ospec, ospec],
        out_shape=[jax.ShapeDtypeStruct(oshape, BF16), jax.ShapeDtypeStruct(oshape, BF16)],
        scratch_shapes=[pltpu.VMEM((n_chains, tk, tq), F32), pltpu.VMEM((n_chains, tk, tq), F32)],
        compiler_params=pltpu.CompilerParams(
            dimension_semantics=("arbitrary", "arbitrary", "arbitrary"),
            vmem_limit_bytes=_vmem_limit(block_bytes, 2 * n_chains * tk * tq * 4, temp_bytes)),
        name="attn",
    )(*args)
    if not transpose_out:
        of, om = jnp.swapaxes(of, 1, 2), jnp.swapaxes(om, 1, 2)
    return of, om


def _ffn_kernel(*refs, d_ff, fc, flat_group):
    if flat_group:
        (h_ref, of_ref, om_ref, g_ref, left_ref, wof_ref, wom_ref, wout_ref, gffn_ref, wup_ref,
         cw_ref, cb_ref, wdown_ref, gfin_ref, y_ref, u_ref, l1_ref, l2_ref, _) = refs
        l1_ref[...] = jnp.zeros_like(l1_ref)
        l2_ref[...] = jnp.zeros_like(l2_ref)
        for g in range(left_ref.shape[0]):
            r0 = g * flat_group
            l1_ref[r0:r0 + 1, :] = left_ref[g, 1:2, :]
            l2_ref[r0:r0 + 1, :] = left_ref[g, 0:1, :]
            l2_ref[r0 + 1:r0 + 2, :] = left_ref[g, 1:2, :]
        ub_ref = refs[-1]
        ub_ref[0:HALO, :] = jnp.zeros((HALO, ub_ref.shape[1]), F32)
    else:
        (h_ref, of_ref, om_ref, g_ref, left_ref, wof_ref, wom_ref, wout_ref, gffn_ref, wup_ref,
         cw_ref, cb_ref, wdown_ref, gfin_ref, y_ref, u_ref, ub_ref) = refs

        @pl.when(pl.program_id(1) == 0)
        def _():
            ub_ref[HALO - (CONV_W - 1):HALO, :] = left_ref[0]

    ts, d = h_ref.shape[1], h_ref.shape[2]
    ya = jnp.dot(of_ref[0], wof_ref[...], preferred_element_type=F32)
    yb = jnp.dot(om_ref[0], wom_ref[...], preferred_element_type=F32)
    mix = g_ref[0, :, :d] * ya + g_ref[0, :, d:] * yb
    h1 = h_ref[0] + jnp.dot(mix.astype(BF16), wout_ref[...], preferred_element_type=F32)
    xn = _rms(h1, gffn_ref[...]).astype(BF16)

    if flat_group:
        row = lax.broadcasted_iota(jnp.int32, (ts, 1), 0)
        keep1 = ((row & (flat_group - 1)) >= 1).astype(F32)
        keep2 = ((row & (flat_group - 1)) >= 2).astype(F32)

    def up(c0):
        return jnp.dot(xn, wup_ref[:, c0:c0 + fc], preferred_element_type=F32)

    def conv(u, c0):
        cols = slice(c0, c0 + fc)
        ub_ref[HALO:HALO + ts, cols] = u
        u1 = ub_ref[HALO - 1:HALO - 1 + ts, cols]
        u2 = ub_ref[HALO - 2:HALO - 2 + ts, cols]
        if flat_group:
            u_ref[0, :, cols] = u
            u1 = u1 * keep1 + l1_ref[:, cols]
            u2 = u2 * keep2 + l2_ref[:, cols]
        return cb_ref[:, cols] + u2 * cw_ref[0:1, cols] + u1 * cw_ref[1:2, cols] + u * cw_ref[2:3, cols]

    n_chunks = d_ff // fc
    acc = jnp.zeros((ts, d), F32)
    ups = [(up(c * fc), up(d_ff + c * fc)) for c in range(min(UP_AHEAD, n_chunks))]
    for c in range(n_chunks):
        u_gate, u_val = ups.pop(0)
        if c + UP_AHEAD < n_chunks:
            ups.append((up((c + UP_AHEAD) * fc), up(d_ff + (c + UP_AHEAD) * fc)))
        act = (jax.nn.silu(conv(u_gate, c * fc)) * conv(u_val, d_ff + c * fc)).astype(BF16)
        acc = acc + jnp.dot(act, wdown_ref[c * fc:(c + 1) * fc, :], preferred_element_type=F32)
    y_ref[0] = _rms(h1 + acc, gfin_ref[...])
    if not flat_group:
        tail = ub_ref[HALO + ts - (CONV_W - 1):HALO + ts, :]
        u_ref[0] = tail
        ub_ref[HALO - (CONV_W - 1):HALO, :] = tail


def _merge_ffn(h, of, om, gates, left, wts, *, ts, flat_group=0):
    B, n, d = h.shape
    d_ff = wts['w_down'].shape[0]
    fc = MXU_DIM
    assert n % ts == 0 and d_ff % fc == 0
    row = lambda w: pl.BlockSpec((1, ts, w), lambda b, t: (b, t, 0))
    consts = [wts[k] for k in ('w_o_fox', 'w_o_mla', 'w_out', 'g_ffn', 'w_up', 'conv_w', 'conv_b', 'w_down', 'g_fin')]
    const_specs = [_const_spec(c.shape) for c in consts]
    in_specs = [row(d), row(of.shape[2]), row(om.shape[2]), row(2 * d)]
    args = [h, of, om, gates, left]
    if flat_group:
        assert B == 1 and ts % flat_group == 0 and left.shape[0] * flat_group == n
        gpt = ts // flat_group
        in_specs.append(pl.BlockSpec((gpt, CONV_W - 1, 2 * d_ff), lambda b, t: (t, 0, 0)))
        u_shape, u_spec = (1, n, 2 * d_ff), row(2 * d_ff)
        scratch = [pltpu.VMEM((ts, 2 * d_ff), F32), pltpu.VMEM((ts, 2 * d_ff), F32)]
        const_extra = 2 * ts * 2 * d_ff * 4 + 2 * gpt * 8 * 2 * d_ff * 4
    else:
        bl = left.shape[0]
        in_specs.append(pl.BlockSpec((1, CONV_W - 1, 2 * d_ff), lambda b, t: (b if bl > 1 else 0, 0, 0)))
        u_shape = (B, CONV_W - 1, 2 * d_ff)
        u_spec = pl.BlockSpec((1, CONV_W - 1, 2 * d_ff), lambda b, t: (b, 0, 0))
        scratch = []
        const_extra = 0
    scratch.append(pltpu.VMEM((HALO + ts, 2 * d_ff), F32))
    const_extra += (HALO + ts) * 2 * d_ff * 4
    u_rows = ts if flat_group else 8
    block_bytes = ts * (d * 4 + (of.shape[2] + om.shape[2]) * 2 + 2 * d * 4 + d * 4) + u_rows * 2 * d_ff * 4
    const_bytes = sum(c.size * c.dtype.itemsize for c in consts) + const_extra
    temp_bytes = ts * d * 4 * 8 + ts * fc * 4 * 16
    y, u = pl.pallas_call(
        functools.partial(_ffn_kernel, d_ff=d_ff, fc=fc, flat_group=flat_group),
        grid=(B, n // ts), in_specs=in_specs + const_specs,
        out_specs=[row(d), u_spec],
        out_shape=[jax.ShapeDtypeStruct((B, n, d), F32), jax.ShapeDtypeStruct(u_shape, F32)],
        scratch_shapes=scratch,
        compiler_params=pltpu.CompilerParams(
            dimension_semantics=("arbitrary", "arbitrary"),
            vmem_limit_bytes=_vmem_limit(block_bytes, const_bytes, temp_bytes)),
        name="ffn",
    )(*args, *consts)
    return y, u


def _rope_tables(pos):
    half = MLA_ROPE // 2
    inv = ROPE_BASE ** (-jnp.arange(0, MLA_ROPE, 2, dtype=F32) / MLA_ROPE)
    ang = pos.astype(F32)[:, None] * inv[None, :]
    cos, sin = jnp.cos(ang), jnp.sin(ang)
    zero = jnp.zeros_like(sin)
    reps = LANES // MLA_ROPE
    ct = jnp.tile(jnp.concatenate([cos, cos], axis=1), (1, reps))
    sa = jnp.tile(jnp.concatenate([-sin, zero], axis=1), (1, reps))
    sb = jnp.tile(jnp.concatenate([zero, sin], axis=1), (1, reps))
    assert half * 2 == MLA_ROPE
    return ct, sa, sb


def _lower_tri(n, group):
    i = np.arange(n)
    m = (i[None, :] <= i[:, None]) & ((i[None, :] // group) == (i[:, None] // group))
    return jnp.asarray(m, BF16)


def _prepare_weights(norm_mix_g, w_in, b_forget, mla_q_norm_g, w_uq, mla_kv_norm_g, w_ukv, w_o_fox, w_o_mla, w_out,
                     norm_ffn_g, w_up, conv_w, conv_b, w_down, norm_final_g):
    d = w_in.shape[0]
    q_lora, kv_lora = mla_q_norm_g.shape[0], mla_kv_norm_g.shape[0]
    off_ff = 3 * FOX_W
    off_cq = off_ff + FOX_HEADS
    off_ckv = off_cq + q_lora
    off_kr = off_ckv + kv_lora
    off_gate = off_kr + MLA_ROPE
    pad = LANES - F_PARTS * FOX_HEADS
    w_ff = jnp.pad(jnp.tile(w_in[:, off_ff:off_cq], (1, F_PARTS)), ((0, 0), (0, pad)))
    w_a = jnp.concatenate([
        w_in[:, :off_ff], w_ff, w_in[:, off_cq:off_kr],
        jnp.tile(w_in[:, off_kr:off_gate], (1, LANES // MLA_ROPE)),
        w_in[:, off_gate:]], axis=1).astype(BF16)
    uq = w_uq.reshape(q_lora, MLA_HEADS, MLA_QK)
    w_uq_p = jnp.concatenate([uq[:, :, :MLA_NOPE].reshape(q_lora, -1), uq[:, :, MLA_NOPE:].reshape(q_lora, -1)], axis=1)
    ukv = w_ukv.reshape(kv_lora, MLA_HEADS, MLA_NOPE + MLA_V)
    w_ukv_p = jnp.concatenate([ukv[:, :, :MLA_NOPE].reshape(kv_lora, -1), ukv[:, :, MLA_NOPE:].reshape(kv_lora, -1)], axis=1)
    return dict(
        g_mix=norm_mix_g.reshape(1, d), w_a=w_a,
        b_exp=jnp.pad(jnp.tile(b_forget, F_PARTS), (0, pad)).reshape(1, LANES),
        g_q=mla_q_norm_g.reshape(1, q_lora), w_uq=w_uq_p.astype(BF16),
        g_kv=mla_kv_norm_g.reshape(1, kv_lora), w_ukv=w_ukv_p.astype(BF16),
        w_o_fox=w_o_fox.astype(BF16), w_o_mla=w_o_mla.astype(BF16), w_out=w_out.astype(BF16),
        g_ffn=norm_ffn_g.reshape(1, d), w_up=w_up.astype(BF16), conv_w=conv_w, conv_b=conv_b.reshape(1, -1),
        w_down=w_down.astype(BF16), g_fin=norm_final_g.reshape(1, d))


PROJ_ROWS = 512
FFN_ROWS = 256
PROMPT_TQ = 256
PROMPT_PAIRS = 4
SMALL_ROWS = LANES
SMALL_PAIRS = 4


def kernel(x_prompt, x_sample, cache_fox_k, cache_fox_v, cache_fox_logf, cache_mla_ckv, cache_mla_krope, state_ffn_conv, meta_tokens, norm_mix_g, w_in, b_forget, mla_q_norm_g, w_uq, mla_kv_norm_g, w_ukv, w_o_fox, w_o_mla, w_out, norm_ffn_g, w_up, conv_w, conv_b, w_down, norm_final_g):
    assert w_in.shape[0] == 1, "single-layer model"
    B, S, d = x_prompt.shape
    Bs, ns, _ = x_sample.shape
    P = cache_fox_k.shape[2]
    n_meta = meta_tokens.shape[0]
    assert n_meta == N_META == ns
    wts = _prepare_weights(norm_mix_g[0], w_in[0], b_forget[0], mla_q_norm_g[0], w_uq[0], mla_kv_norm_g[0], w_ukv[0],
                           w_o_fox[0], w_o_mla[0], w_out[0], norm_ffn_g[0], w_up[0], conv_w[0], conv_b[0], w_down[0],
                           norm_final_g)
    d_ff2 = w_up.shape[2]

    r_pad = -(-(Bs + 1) * ns // SMALL_ROWS) * SMALL_ROWS
    n_groups = r_pad // ns
    n_fill = r_pad - (Bs + 1) * ns
    rows = jnp.concatenate([x_sample.reshape(Bs * ns, d), jnp.zeros((n_fill, d), x_sample.dtype),
                            meta_tokens.astype(x_sample.dtype)], axis=0)[None]
    pos_small = jnp.concatenate([jnp.tile(P + jnp.arange(ns), Bs), jnp.zeros((n_fill,), jnp.int32), jnp.arange(n_meta)])
    zero_f = jnp.zeros((1, 8, LANES), F32)
    sm = _project(rows, _rope_tables(pos_small), zero_f, wts, _lower_tri(r_pad, ns), ts=r_pad, cs=r_pad)
    sm_b = {k: (jnp.swapaxes(v.reshape(v.shape[1], n_groups, ns), 0, 1) if k in TRANSPOSED
                else v.reshape(n_groups, ns, v.shape[-1])) for k, v in sm.items() if k != 'ftot'}
    live = lambda a: jnp.concatenate([a[:Bs], a[n_groups - 1:]], axis=0)

    feat_major = lambda a: jnp.moveaxis(a[0], 1, -1).reshape(Bs, -1, P)
    prefix_s = _prefix_operands(
        feat_major(cache_fox_k), feat_major(cache_fox_v), feat_major(cache_fox_logf), cache_mla_ckv[0],
        feat_major(cache_mla_krope), wts['w_ukv'], _lower_tri(MXU_DIM, MXU_DIM), cs=MXU_DIM)
    of_s, om_s = _attention({k: live(v) for k, v in sm_b.items()}, prefix_s, tq=ns, tk=ns, tkp=P, pps=SMALL_PAIRS, n_pref=Bs)

    def spread(a):
        return jnp.concatenate([a[:Bs].reshape(Bs * ns, -1), jnp.zeros((n_fill, a.shape[-1]), a.dtype), a[Bs]], axis=0)[None]

    left_groups = jnp.concatenate([state_ffn_conv[0], jnp.zeros((n_groups - Bs, CONV_W - 1, d_ff2), F32)], axis=0)
    y_s, u_s = _merge_ffn(rows, spread(of_s), spread(om_s), sm['gate'], left_groups, wts, ts=SMALL_ROWS, flat_group=ns)
    y_sample = y_s[0, :Bs * ns].reshape(Bs, ns, d)
    u_groups = u_s.reshape(n_groups, ns, d_ff2)
    conv_s = u_groups[:Bs, ns - (CONV_W - 1):]
    left_meta = u_groups[n_groups - 1:, ns - (CONV_W - 1):]

    meta_ops = {k: sm_b[k][n_groups - 1:] for k in ('kfa', 'vf', 'kc', 'vm')}
    pos_p = n_meta + jnp.arange(S)
    pr = _project(x_prompt, _rope_tables(pos_p), sm['ftot'], wts, _lower_tri(MXU_DIM, MXU_DIM), ts=PROJ_ROWS,
                  cs=MXU_DIM, head_rows={k: sm_b[k][n_groups - 1] for k in CACHE_KEYS})
    of_p, om_p = _attention(pr, meta_ops, tq=PROMPT_TQ, tk=PROMPT_TQ, tkp=n_meta, pps=PROMPT_PAIRS)
    y_prompt, conv_p = _merge_ffn(x_prompt, of_p, om_p, pr['gate'], left_meta, wts, ts=FFN_ROWS)
    caches_p = [pr[k].reshape((1, B, n_meta + S) + ((FOX_HEADS, FOX_HEAD_DIM) if k in ('fk', 'fv') else pr[k].shape[3:]))
                for k in CACHE_KEYS]

    def sample_rows(name, tail):
        return sm_b[name][:Bs].reshape((1, Bs, ns) + tail)

    hd = (FOX_HEADS, FOX_HEAD_DIM)
    return (y_prompt, y_sample, *caches_p, conv_p[None],
            sample_rows('fk', hd), sample_rows('fv', hd), sample_rows('logf', (FOX_HEADS,)),
            sample_rows('ckv', (cache_mla_ckv.shape[-1],)), sample_rows('kr', (MLA_ROPE,)), conv_s[None])
```

```python
import functools
import math

import jax
import jax.numpy as jnp
import numpy as np
from jax import lax
from jax.experimental import pallas as pl
from jax.experimental.pallas import tpu as pltpu

F32 = jnp.float32
BF16 = jnp.bfloat16

N_META = 16
CHUNK = 64
EPS = 1e-6
NEG = -1e30
FOX_HEADS = 8
FOX_HEAD_DIM = 64
MLA_HEADS = 8
MLA_NOPE = 64
MLA_ROPE = 32
MLA_V = 64
MLA_QK = MLA_NOPE + MLA_ROPE
ROPE_BASE = 10000.0
CONV_W = 3

FOX_W = FOX_HEADS * FOX_HEAD_DIM
LOG2E = math.log2(math.e)
FOX_QSCALE = LOG2E / math.sqrt(FOX_HEAD_DIM)
MLA_QSCALE = LOG2E / math.sqrt(MLA_QK)

LANES = 128
MXU_DIM = 256
VMEM_BYTES_V7X = 64 * 1024 * 1024
F_PARTS = 3
HEADS_PER_PAIR = LANES // FOX_HEAD_DIM
N_PAIRS = FOX_HEADS // HEADS_PER_PAIR
PAIR_W = 2 * LANES
HALO = 8
ONES_ROWS = 16
GATE_CHUNKS = 4
UP_AHEAD = 3

C_FQ = 0
C_FK = C_FQ + FOX_W
C_FV = C_FK + FOX_W
C_FF = C_FV + FOX_W
C_CQ = C_FF + LANES


def _rms(x, g):
    return x * lax.rsqrt(jnp.mean(x * x, axis=-1, keepdims=True) + EPS) * g


def _split3(x):
    hi = x.astype(BF16)
    r = x - hi.astype(F32)
    mid = r.astype(BF16)
    lo = (r - mid.astype(F32)).astype(BF16)
    return hi, mid, lo


def _cumsum_rows(lf, tri, carry, cs):
    hi, mid, lo = _split3(lf)
    pieces = []
    for c in range(lf.shape[0] // cs):
        sl = slice(c * cs, (c + 1) * cs)
        fc = (jnp.dot(tri, hi[sl], preferred_element_type=F32)
              + jnp.dot(tri, mid[sl], preferred_element_type=F32)
              + jnp.dot(tri, lo[sl], preferred_element_type=F32)) + carry
        carry = fc[cs - 1:cs, :]
        pieces.append(fc)
    f = pieces[0] if len(pieces) == 1 else jnp.concatenate(pieces, axis=0)
    return f, carry


def _forget_columns(f):
    hi, mid, lo = _split3(f)
    lane = lax.broadcasted_iota(jnp.int32, f.shape, 1)
    zero = jnp.zeros_like(hi)
    return jnp.where(lane < FOX_HEADS, hi,
                     jnp.where(lane < 2 * FOX_HEADS, mid,
                               jnp.where(lane < F_PARTS * FOX_HEADS, lo, zero)))


def _store_pairs(ref, main, extra):
    for j in range(N_PAIRS):
        ref[0, :, PAIR_W * j:PAIR_W * j + LANES] = main[:, LANES * j:LANES * (j + 1)]
        ref[0, :, PAIR_W * j + LANES:PAIR_W * (j + 1)] = extra


def _store_cache(ref, val, prev_ref=None):
    lead = (0,) * (len(ref.shape) - 2)
    width = prev_ref.shape[1] if val is None else val.shape[1]
    heads = width // ref.shape[-1]

    def put(row0, v):
        rows = v.shape[0]
        if heads == 1:
            ref[lead + (slice(row0, row0 + rows), slice(None))] = v
        else:
            dh = ref.shape[-1]
            for h in range(heads):
                ref[lead + (pl.ds(row0 * heads + h, rows, stride=heads), slice(None))] = v[:, h * dh:(h + 1) * dh]

    if prev_ref is None:
        assert ref.shape[-2] == val.shape[0] * heads
        put(0, val)
        return
    r0 = prev_ref.shape[0]
    put(0, prev_ref[...])
    if val is not None:
        assert ref.shape[-2] == val.shape[0] * heads
        put(r0, val[:val.shape[0] - r0, :])
        prev_ref[...] = val[val.shape[0] - r0:, :]


def _rope(x, ct, sa, sb):
    return x * ct + pltpu.roll(x, LANES - MLA_ROPE // 2, 1) * sa + pltpu.roll(x, MLA_ROPE // 2, 1) * sb


def _proj_kernel(*refs, nt, with_head_rows, **kw):
    n_in = 13 + (len(CACHE_KEYS) if with_head_rows else 0)
    cache_refs = refs[n_in:n_in + len(CACHE_KEYS)]
    carry_ref = refs[n_in + 14]
    prev = dict(zip(CACHE_KEYS, refs[n_in + 15:])) if with_head_rows else dict.fromkeys(CACHE_KEYS)
    t = pl.program_id(1)

    @pl.when(t == 0)
    def _():
        carry_ref[...] = refs[4][0]
        if with_head_rows:
            for k, head_ref in zip(CACHE_KEYS, refs[13:n_in]):
                prev[k][...] = head_ref[...]

    tile_refs = (refs[:13], refs[n_in:n_in + 15], prev)
    if not with_head_rows:
        _proj_tile(*tile_refs, **kw)
        return

    @pl.when(t < nt - 1)
    def _():
        _proj_tile(*tile_refs, **kw)

    @pl.when(t == nt - 1)
    def _():
        _proj_tile(*tile_refs, gates=False, **kw)

    @pl.when(t == nt)
    def _():
        _proj_tile(*tile_refs, main=False, **kw)
        for k, ref in zip(CACHE_KEYS, cache_refs):
            _store_cache(ref, None, prev[k])


def _proj_tile(in_refs, out_refs, prev, *, cs, q_lora, kv_lora, d_model, main=True, gates=True):
    (x_ref, ct_ref, sa_ref, sb_ref, finit_ref, gmix_ref, w_ref, bexp_ref, gq_ref, wuq_ref,
     gkv_ref, wukv_ref, tri_ref) = in_refs
    (fk_ref, fv_ref, logf_ref, ckv_ref, kr_ref, qf_ref, kfa_ref, vf_ref, qn_ref, qr_ref,
     kc_ref, vm_ref, gate_ref, ftot_ref, carry_ref) = out_refs
    c_ckv = C_CQ + q_lora
    c_kr = c_ckv + kv_lora
    c_gate = c_kr + LANES
    xb = _rms(x_ref[0], gmix_ref[...]).astype(BF16)

    def proj(a, b):
        return jnp.dot(xb, w_ref[:, a:b], preferred_element_type=F32)

    def merge_gates():
        gw = 2 * d_model // GATE_CHUNKS
        for c in range(GATE_CHUNKS):
            gate_ref[0, :, c * gw:(c + 1) * gw] = jax.nn.sigmoid(proj(c_gate + c * gw, c_gate + (c + 1) * gw))

    if not main:
        merge_gates()
        return

    ct, sa, sb = ct_ref[...], sa_ref[...], sb_ref[...]
    nope_w = MLA_HEADS * MLA_NOPE
    cq = proj(C_CQ, c_ckv)
    ckv = proj(c_ckv, c_kr)
    ff = proj(C_FF, C_CQ)
    kr = proj(c_kr, c_gate)

    qf_ref[0] = (proj(C_FQ, C_FK) * FOX_QSCALE).T.astype(BF16)
    fk = proj(C_FK, C_FV)
    _store_cache(fk_ref, fk, prev['fk'])
    fv = proj(C_FV, C_FF)
    _store_cache(fv_ref, fv, prev['fv'])
    vf_ref[0] = fv.T.astype(BF16)

    cqn = _rms(cq, gq_ref[...]).astype(BF16)
    ckvn = _rms(ckv, gkv_ref[...])
    _store_cache(ckv_ref, ckvn, prev['ckv'])
    q = jnp.dot(cqn, wuq_ref[...], preferred_element_type=F32) * MLA_QSCALE
    kv = jnp.dot(ckvn.astype(BF16), wukv_ref[...], preferred_element_type=F32)

    lf = jax.nn.log_sigmoid(ff + bexp_ref[...])
    _store_cache(logf_ref, lf[:, :FOX_HEADS], prev['logf'])
    f, carry = _cumsum_rows(lf, tri_ref[...], carry_ref[0:1, :], cs)
    carry_ref[0:1, :] = carry
    ftot_ref[0] = jnp.broadcast_to(carry, ftot_ref.shape[1:])

    if gates:
        merge_gates()

    _store_pairs(kfa_ref, fk.astype(BF16), _forget_columns(f * LOG2E))
    qn_ref[0] = q[:, :nope_w].T.astype(BF16)
    for c in range(MLA_HEADS * MLA_ROPE // LANES):
        qr_ref[0, LANES * c:LANES * (c + 1), :] = _rope(
            q[:, nope_w + LANES * c:nope_w + LANES * (c + 1)], ct, sa, sb).T.astype(BF16)
    vm_ref[0] = kv[:, nope_w:].T.astype(BF16)
    kr4 = _rope(kr, ct, sa, sb)
    _store_cache(kr_ref, kr4[:, :MLA_ROPE], prev['kr'])
    _store_pairs(kc_ref, kv[:, :nope_w].astype(BF16), kr4.astype(BF16))


TRANSPOSED = ('qf', 'vf', 'qn', 'qr', 'vm')
CACHE_KEYS = ('fk', 'fv', 'logf', 'ckv', 'kr')


def _const_spec(shape):
    return pl.BlockSpec(shape, lambda *_: (0,) * len(shape), pipeline_mode=pl.Buffered(1))


def _vmem_limit(block_bytes, const_bytes, temp_bytes):
    need = 2 * block_bytes + const_bytes + temp_bytes
    assert need < VMEM_BYTES_V7X, need
    return int(need)


def _project(x, tables, finit, wts, tri, *, ts, cs, head_rows=None):
    B, n, d = x.shape
    assert n % ts == 0 and ts % cs == 0
    nt = n // ts
    q_lora = wts['g_q'].shape[1]
    kv_lora = wts['g_kv'].shape[1]
    ct, sa, sb = tables
    tile = (lambda t: jnp.minimum(t, nt - 1)) if head_rows is not None else (lambda t: t)
    row = lambda w: pl.BlockSpec((1, ts, w), lambda b, t: (b, tile(t), 0))
    tab = pl.BlockSpec((ts, LANES), lambda b, t: (tile(t), 0))
    in_specs = [row(d), tab, tab, tab,
                pl.BlockSpec((1, 8, LANES), lambda b, t: (0, 0, 0)),
                _const_spec(wts['g_mix'].shape), _const_spec(wts['w_a'].shape), _const_spec(wts['b_exp'].shape),
                _const_spec(wts['g_q'].shape), _const_spec(wts['w_uq'].shape), _const_spec(wts['g_kv'].shape),
                _const_spec(wts['w_ukv'].shape), _const_spec(tri.shape)]
    widths = dict(fk=(FOX_W, F32), fv=(FOX_W, F32), logf=(FOX_HEADS, F32), ckv=(kv_lora, F32), kr=(MLA_ROPE, F32),
                  qf=(FOX_W, BF16), kfa=(N_PAIRS * PAIR_W, BF16), vf=(FOX_W, BF16),
                  qn=(MLA_HEADS * MLA_NOPE, BF16), qr=(MLA_HEADS * MLA_ROPE, BF16),
                  kc=(N_PAIRS * PAIR_W, BF16), vm=(MLA_HEADS * MLA_V, BF16), gate=(2 * d, F32))
    col = lambda w: pl.BlockSpec((1, w, ts), lambda b, t: (b, 0, tile(t)))
    out_shape = [jax.ShapeDtypeStruct((B, w, n) if k in TRANSPOSED else (B, n, w), dt) for k, (w, dt) in widths.items()]
    out_specs = [col(w) if k in TRANSPOSED else row(w) for k, (w, _) in widths.items()]
    args = [x, ct, sa, sb, finit, wts['g_mix'], wts['w_a'], wts['b_exp'], wts['g_q'], wts['w_uq'], wts['g_kv'],
            wts['w_ukv'], tri]
    scratch = [pltpu.VMEM((8, LANES), F32)]
    if head_rows is not None:
        r0 = head_rows[CACHE_KEYS[0]].shape[0]
        assert r0 % 8 == 0 and ts % 8 == 0 and r0 < ts
        for idx, k in enumerate(CACHE_KEYS):
            heads, w = (FOX_HEADS, FOX_HEAD_DIM) if k in ('fk', 'fv') else (1, widths[k][0])
            out_shape[idx] = jax.ShapeDtypeStruct((1, B, (r0 + n) * heads, w), F32)
            out_specs[idx] = pl.BlockSpec((1, 1, ts * heads, w), lambda b, t: (0, b, t, 0))
            args.append(head_rows[k])
            in_specs.append(_const_spec(head_rows[k].shape))
            scratch.append(pltpu.VMEM(head_rows[k].shape, F32))
    out_shape.append(jax.ShapeDtypeStruct((B, 8, LANES), F32))
    out_specs.append(pl.BlockSpec((1, 8, LANES), lambda b, t: (b, 0, 0)))
    block_bytes = ts * (d * 4 + 3 * LANES * 4 + sum(w * jnp.dtype(dt).itemsize for w, dt in widths.values()))
    if head_rows is not None:
        block_bytes += 2 * ts * FOX_HEADS * LANES * 4
    const_bytes = sum(int(np.prod(wts[k].shape)) * wts[k].dtype.itemsize for k in ('w_a', 'w_uq', 'w_ukv')) + tri.size * 2
    temp_bytes = ts * (2 * d + 2 * d) * 4 * 2
    outs = pl.pallas_call(
        functools.partial(_proj_kernel, cs=cs, q_lora=q_lora, kv_lora=kv_lora, d_model=d, nt=nt,
                          with_head_rows=head_rows is not None),
        grid=(B, nt + (head_rows is not None)), in_specs=in_specs, out_specs=out_specs, out_shape=out_shape,
        scratch_shapes=scratch,
        compiler_params=pltpu.CompilerParams(
            dimension_semantics=("arbitrary", "arbitrary"),
            vmem_limit_bytes=_vmem_limit(block_bytes, const_bytes, temp_bytes)),
        name="proj",
    )(*args)
    res = dict(zip(widths.keys(), outs[:-1]))
    res['ftot'] = outs[-1]
    return res


def _prefix_kernel(ckt_ref, cvt_ref, lft_ref, ckv_ref, krt_ref, wukv_ref, tri_ref,
                   kfa_ref, vf_ref, kc_ref, vm_ref, *, cs):
    P = ckt_ref.shape[2]
    lft = lft_ref[0]
    lf_exp = jnp.concatenate([lft] * F_PARTS + [jnp.zeros((LANES - F_PARTS * FOX_HEADS, P), F32)], axis=0).T
    f, total = _cumsum_rows(lf_exp, tri_ref[...], jnp.zeros((1, LANES), F32), cs)
    _store_pairs(kfa_ref, ckt_ref[0].T.astype(BF16), _forget_columns((f - total) * LOG2E))
    vf_ref[0] = cvt_ref[0].astype(BF16)
    kv = jnp.dot(ckv_ref[0].astype(BF16), wukv_ref[...], preferred_element_type=F32)
    nope_w = MLA_HEADS * MLA_NOPE
    vm_ref[0] = kv[:, nope_w:].T.astype(BF16)
    kr4 = jnp.concatenate([krt_ref[0]] * (LANES // MLA_ROPE), axis=0).T
    _store_pairs(kc_ref, kv[:, :nope_w].astype(BF16), kr4.astype(BF16))


def _prefix_operands(ckt, cvt, lft, cckv, krt, w_ukv, tri, *, cs):
    B, _, P = ckt.shape
    row = lambda r, w: pl.BlockSpec((1, r, w), lambda b: (b, 0, 0))
    widths = dict(kfa=N_PAIRS * PAIR_W, vf=FOX_W, kc=N_PAIRS * PAIR_W, vm=MLA_HEADS * MLA_V)
    ins = (ckt, cvt, lft, cckv, krt)
    block_bytes = sum(max(a.shape[1], 8) * max(a.shape[2], LANES) * 4 for a in ins) + P * sum(widths.values()) * 2
    outs = pl.pallas_call(
        functools.partial(_prefix_kernel, cs=cs),
        grid=(B,),
        in_specs=[row(a.shape[1], a.shape[2]) for a in ins] + [_const_spec(w_ukv.shape), _const_spec(tri.shape)],
        out_specs=[row(w, P) if k in TRANSPOSED else row(P, w) for k, w in widths.items()],
        out_shape=[jax.ShapeDtypeStruct((B, w, P) if k in TRANSPOSED else (B, P, w), BF16) for k, w in widths.items()],
        compiler_params=pltpu.CompilerParams(
            dimension_semantics=("arbitrary",),
            vmem_limit_bytes=_vmem_limit(block_bytes, w_ukv.size * 2 + tri.size * 2, P * 1024 * 4 * 4)),
        name="prefix",
    )(*ins, w_ukv, tri)
    return dict(zip(widths.keys(), outs))


def _scores(ks, qts):
    return tuple(jnp.dot(k, qt, preferred_element_type=F32) for k, qt in zip(ks, qts))


def _online_update(states, scores, vts, masks=None, extra=None):
    mids = []
    for c, ((m, l, acc), s) in enumerate(zip(states, scores)):
        if masks is not None:
            s = jnp.where(masks[c], s, NEG)
        n_extra = 0
        if extra is not None:
            n_extra = extra[0][c].shape[0]
            s = jnp.concatenate([extra[0][c], s], axis=0)
        m_new = jnp.maximum(m, jnp.max(s, axis=0, keepdims=True))
        alpha = jnp.exp2(m - m_new)
        p = jnp.exp2(s - m_new)
        mids.append((m_new, alpha * l, alpha * acc, p.astype(BF16), n_extra))
    out = []
    for c, ((m_new, l, acc, p, n_extra), vt) in enumerate(zip(mids, vts)):
        hd = vt.shape[0]
        ones = lambda n: jnp.ones((ONES_ROWS, n), BF16)
        r = jnp.dot(jnp.concatenate([vt, ones(vt.shape[1])], axis=0), p[n_extra:], preferred_element_type=F32)
        if n_extra:
            r = r + jnp.dot(jnp.concatenate([extra[1][c], ones(n_extra)], axis=0), p[:n_extra],
                            preferred_element_type=F32)
        out.append((m_new, l + r[hd:hd + 1], acc + r[:hd]))
    return tuple(out)


def _attn_kernel(*refs, tq, tk, tkp, nq, pps, has_prefix, n_pref, transpose_out):
    if has_prefix:
        (qf_ref, qn_ref, qr_ref, kfa_ref, vf_ref, kc_ref, vm_ref,
         pkfa_ref, pvf_ref, pkc_ref, pvm_ref, of_ref, om_ref, sa_ref, sb_ref) = refs
    else:
        qf_ref, qn_ref, qr_ref, kfa_ref, vf_ref, kc_ref, vm_ref, of_ref, om_ref, sa_ref, sb_ref = refs
        pkfa_ref = pvf_ref = pkc_ref = pvm_ref = None
    b, j, i = pl.program_id(0), pl.program_id(1), pl.program_id(2)
    pref_bias = None if (not has_prefix or n_pref is None) else jnp.where(b < n_pref, 0.0, NEG)

    frow = lax.broadcasted_iota(jnp.int32, (LANES, tq), 0)
    zero_half = jnp.zeros((FOX_HEAD_DIM, tq), BF16)
    rope_groups = LANES // MLA_ROPE
    rope_rows_per_step = pps * HEADS_PER_PAIR * MLA_ROPE
    chains = []
    for mixer in range(2):
        for p in range(pps):
            for s in range(HEADS_PER_PAIR):
                h = HEADS_PER_PAIR * (pps * j + p) + s
                vrows = slice(p * LANES + s * FOX_HEAD_DIM, p * LANES + (s + 1) * FOX_HEAD_DIM)
                own = [zero_half] * HEADS_PER_PAIR
                own[s] = (qf_ref if mixer == 0 else qn_ref)[0, vrows, :]
                if mixer == 0:
                    extra = jnp.where((frow < F_PARTS * FOX_HEADS) & ((frow & (FOX_HEADS - 1)) == h),
                                      -1.0, 0.0).astype(BF16)
                else:
                    r0 = (p * HEADS_PER_PAIR * MLA_ROPE // LANES) * LANES if rope_rows_per_step > LANES else 0
                    qr = qr_ref[0, r0:r0 + LANES, :]
                    extra = jnp.where((frow >> int(math.log2(MLA_ROPE))) == (h & (rope_groups - 1)), qr, jnp.zeros_like(qr))
                chains.append((jnp.concatenate(own + [extra], axis=0), slice(p * PAIR_W, (p + 1) * PAIR_W), vrows, mixer))
    qts = [c[0] for c in chains]
    key_refs, val_refs = (kfa_ref, kc_ref), (vf_ref, vm_ref)
    pkey_refs, pval_refs = (pkfa_ref, pkc_ref), (pvf_ref, pvm_ref)

    def scores(krefs, rows):
        return _scores([krefs[c[3]][0, rows, c[1]] for c in chains], qts)

    def update(sts, sc, vrefs, rows, **kw):
        return _online_update(sts, sc, [vrefs[c[3]][0, c[2], rows] for c in chains], **kw)

    assert tq == tk
    tile_rows = (lambda t: pl.ds(pl.multiple_of(t * tk, tk), tk)) if nq > 1 else (lambda t: slice(0, tk))

    def put_scores(ref, t):
        for c, s in enumerate(scores(key_refs, tile_rows(t))):
            ref[c] = s

    def step(sts, ref, t, **kw):
        return update(sts, [ref[c] for c in range(len(chains))], val_refs, tile_rows(t), **kw)

    prefix = None
    if has_prefix:
        assert tkp == pkfa_ref.shape[1]
        all_rows = slice(0, tkp)
        psc = scores(pkey_refs, all_rows)
        if pref_bias is not None:
            psc = [s + pref_bias for s in psc]
        prefix = (psc, [pval_refs[c[3]][0, c[2], all_rows] for c in chains])
    put_scores(sa_ref, 0)

    def finish(sts, ref):
        krow = lax.broadcasted_iota(jnp.int32, (tq, tq), 0)
        qcol = lax.broadcasted_iota(jnp.int32, (tq, tq), 1)
        chunk_shift = int(math.log2(CHUNK))
        masks = (krow <= qcol, (krow >> chunk_shift) <= (qcol >> chunk_shift))
        sts = step(sts, ref, i, masks=[masks[c[3]] for c in chains], extra=prefix)
        outs = [acc / l for _, l, acc in sts]
        half = len(outs) // 2
        for o_ref, group in ((of_ref, outs[:half]), (om_ref, outs[half:])):
            ot = jnp.concatenate(group, axis=0)
            o_ref[0] = (ot.T if transpose_out else ot).astype(BF16)

    init = (jnp.full((1, tq), NEG, F32), jnp.zeros((1, tq), F32), jnp.zeros((FOX_HEAD_DIM, tq), F32))
    states = tuple(init for _ in chains)
    if nq == 1:
        finish(states, sa_ref)
    else:
        def body(u, sts):
            put_scores(sb_ref, 2 * u + 1)
            sts = step(sts, sa_ref, 2 * u)
            put_scores(sa_ref, 2 * u + 2)
            return step(sts, sb_ref, 2 * u + 1)

        states = lax.fori_loop(0, i >> 1, body, states)

        @pl.when((i & 1) == 0)
        def _():
            finish(states, sa_ref)

        @pl.when((i & 1) == 1)
        def _():
            put_scores(sb_ref, i)
            finish(step(states, sa_ref, i - 1), sb_ref)


def _attention(ops, prefix, *, tq, tk, tkp, pps, n_pref=None):
    B, n, _ = ops['kfa'].shape
    assert n % tq == 0 and tq % tk == 0 and N_PAIRS % pps == 0
    nq = n // tq
    transpose_out = tq % LANES == 0
    rope_w = max(LANES, pps * HEADS_PER_PAIR * MLA_ROPE)
    steps_per_rope_block = rope_w // (pps * HEADS_PER_PAIR * MLA_ROPE)
    qspec = pl.BlockSpec((1, pps * LANES, tq), lambda b, j, i: (b, j, i))
    qrspec = pl.BlockSpec((1, rope_w, tq), lambda b, j, i: (b, j // steps_per_rope_block, i))
    kspec = pl.BlockSpec((1, n, pps * PAIR_W), lambda b, j, i: (b, 0, j))
    vspec = pl.BlockSpec((1, pps * LANES, n), lambda b, j, i: (b, j, 0))
    args = [ops['qf'], ops['qn'], ops['qr'], ops['kfa'], ops['vf'], ops['kc'], ops['vm']]
    in_specs = [qspec, qspec, qrspec, kspec, vspec, kspec, vspec]
    block_bytes = 2 * pps * (3 * tq * LANES + 2 * n * (PAIR_W + LANES) + 2 * tq * LANES)
    if prefix is not None:
        Bp, P, _ = prefix['kfa'].shape
        assert P % tkp == 0
        if Bp == 1:
            pb = lambda b: 0
        elif n_pref is not None:
            pb = lambda b: jnp.minimum(b, n_pref - 1)
        else:
            pb = lambda b: b
        pk = pl.BlockSpec((1, P, pps * PAIR_W), lambda b, j, i: (pb(b), 0, j))
        pv = pl.BlockSpec((1, pps * LANES, P), lambda b, j, i: (pb(b), j, 0))
        args += [prefix['kfa'], prefix['vf'], prefix['kc'], prefix['vm']]
        in_specs += [pk, pv, pk, pv]
        block_bytes += 2 * 2 * pps * P * (PAIR_W + LANES)
    if transpose_out:
        ospec = pl.BlockSpec((1, tq, pps * LANES), lambda b, j, i: (b, i, j))
        oshape = (B, n, N_PAIRS * LANES)
    else:
        ospec = pl.BlockSpec((1, pps * LANES, tq), lambda b, j, i: (b, j, i))
        oshape = (B, N_PAIRS * LANES, n)
    n_chains = 2 * HEADS_PER_PAIR * pps
    tq_pad = max(tq, LANES)
    temp_bytes = n_chains * 3 * tq_pad * max(tk, tkp if prefix is not None else tk) * 4 + 4 * n_chains * tq_pad * LANES * 4
    of, om = pl.pallas_call(
        functools.partial(_attn_kernel, tq=tq, tk=tk, tkp=tkp, has_prefix=prefix is not None, n_pref=n_pref,
                          nq=nq, pps=pps, transpose_out=transpose_out),
        grid=(B, N_PAIRS // pps, nq), in_specs=in_specs, out_specs=[ospec, ospec],
        out_shape=[jax.ShapeDtypeStruct(oshape, BF16), jax.ShapeDtypeStruct(oshape, BF16)],
        scratch_shapes=[pltpu.VMEM((n_chains, tk, tq), F32), pltpu.VMEM((n_chains, tk, tq), F32)],
        compiler_params=pltpu.CompilerParams(
            dimension_semantics=("arbitrary", "arbitrary", "arbitrary"),
            vmem_limit_bytes=_vmem_limit(block_bytes, 2 * n_chains * tk * tq * 4, temp_bytes)),
        name="attn",
    )(*args)
    if not transpose_out:
        of, om = jnp.swapaxes(of, 1, 2), jnp.swapaxes(om, 1, 2)
    return of, om


def _ffn_kernel(*refs, d_ff, fc, flat_group):
    if flat_group:
        (h_ref, of_ref, om_ref, g_ref, left_ref, wof_ref, wom_ref, wout_ref, gffn_ref, wup_ref,
         cw_ref, cb_ref, wdown_ref, gfin_ref, y_ref, u_ref, l1_ref, l2_ref, _) = refs
        l1_ref[...] = jnp.zeros_like(l1_ref)
        l2_ref[...] = jnp.zeros_like(l2_ref)
        for g in range(left_ref.shape[0]):
            r0 = g * flat_group
            l1_ref[r0:r0 + 1, :] = left_ref[g, 1:2, :]
            l2_ref[r0:r0 + 1, :] = left_ref[g, 0:1, :]
            l2_ref[r0 + 1:r0 + 2, :] = left_ref[g, 1:2, :]
        ub_ref = refs[-1]
        ub_ref[0:HALO, :] = jnp.zeros((HALO, ub_ref.shape[1]), F32)
    else:
        (h_ref, of_ref, om_ref, g_ref, left_ref, wof_ref, wom_ref, wout_ref, gffn_ref, wup_ref,
         cw_ref, cb_ref, wdown_ref, gfin_ref, y_ref, u_ref, ub_ref) = refs

        @pl.when(pl.program_id(1) == 0)
        def _():
            ub_ref[HALO - (CONV_W - 1):HALO, :] = left_ref[0]

    ts, d = h_ref.shape[1], h_ref.shape[2]
    ya = jnp.dot(of_ref[0], wof_ref[...], preferred_element_type=F32)
    yb = jnp.dot(om_ref[0], wom_ref[...], preferred_element_type=F32)
    mix = g_ref[0, :, :d] * ya + g_ref[0, :, d:] * yb
    h1 = h_ref[0] + jnp.dot(mix.astype(BF16), wout_ref[...], preferred_element_type=F32)
    xn = _rms(h1, gffn_ref[...]).astype(BF16)

    if flat_group:
        row = lax.broadcasted_iota(jnp.int32, (ts, 1), 0)
        keep1 = ((row & (flat_group - 1)) >= 1).astype(F32)
        keep2 = ((row & (flat_group - 1)) >= 2).astype(F32)

    def up(c0):
        return jnp.dot(xn, wup_ref[:, c0:c0 + fc], preferred_element_type=F32)

    def conv(u, c0):
        cols = slice(c0, c0 + fc)
        ub_ref[HALO:HALO + ts, cols] = u
        u1 = ub_ref[HALO - 1:HALO - 1 + ts, cols]
        u2 = ub_ref[HALO - 2:HALO - 2 + ts, cols]
        if flat_group:
            u_ref[0, :, cols] = u
            u1 = u1 * keep1 + l1_ref[:, cols]
            u2 = u2 * keep2 + l2_ref[:, cols]
        return cb_ref[:, cols] + u2 * cw_ref[0:1, cols] + u1 * cw_ref[1:2, cols] + u * cw_ref[2:3, cols]

    n_chunks = d_ff // fc
    acc = jnp.zeros((ts, d), F32)
    ups = [(up(c * fc), up(d_ff + c * fc)) for c in range(min(UP_AHEAD, n_chunks))]
    for c in range(n_chunks):
        u_gate, u_val = ups.pop(0)
        if c + UP_AHEAD < n_chunks:
            ups.append((up((c + UP_AHEAD) * fc), up(d_ff + (c + UP_AHEAD) * fc)))
        act = (jax.nn.silu(conv(u_gate, c * fc)) * conv(u_val, d_ff + c * fc)).astype(BF16)
        acc = acc + jnp.dot(act, wdown_ref[c * fc:(c + 1) * fc, :], preferred_element_type=F32)
    y_ref[0] = _rms(h1 + acc, gfin_ref[...])
    if not flat_group:
        tail = ub_ref[HALO + ts - (CONV_W - 1):HALO + ts, :]
        u_ref[0] = tail
        ub_ref[HALO - (CONV_W - 1):HALO, :] = tail


def _merge_ffn(h, of, om, gates, left, wts, *, ts, flat_group=0):
    B, n, d = h.shape
    d_ff = wts['w_down'].shape[0]
    fc = MXU_DIM
    assert n % ts == 0 and d_ff % fc == 0
    row = lambda w: pl.BlockSpec((1, ts, w), lambda b, t: (b, t, 0))
    consts = [wts[k] for k in ('w_o_fox', 'w_o_mla', 'w_out', 'g_ffn', 'w_up', 'conv_w', 'conv_b', 'w_down', 'g_fin')]
    const_specs = [_const_spec(c.shape) for c in consts]
    in_specs = [row(d), row(of.shape[2]), row(om.shape[2]), row(2 * d)]
    args = [h, of, om, gates, left]
    if flat_group:
        assert B == 1 and ts % flat_group == 0 and left.shape[0] * flat_group == n
        gpt = ts // flat_group
        in_specs.append(pl.BlockSpec((gpt, CONV_W - 1, 2 * d_ff), lambda b, t: (t, 0, 0)))
        u_shape, u_spec = (1, n, 2 * d_ff), row(2 * d_ff)
        scratch = [pltpu.VMEM((ts, 2 * d_ff), F32), pltpu.VMEM((ts, 2 * d_ff), F32)]
        const_extra = 2 * ts * 2 * d_ff * 4 + 2 * gpt * 8 * 2 * d_ff * 4
    else:
        bl = left.shape[0]
        in_specs.append(pl.BlockSpec((1, CONV_W - 1, 2 * d_ff), lambda b, t: (b if bl > 1 else 0, 0, 0)))
        u_shape = (B, CONV_W - 1, 2 * d_ff)
        u_spec = pl.BlockSpec((1, CONV_W - 1, 2 * d_ff), lambda b, t: (b, 0, 0))
        scratch = []
        const_extra = 0
    scratch.append(pltpu.VMEM((HALO + ts, 2 * d_ff), F32))
    const_extra += (HALO + ts) * 2 * d_ff * 4
    u_rows = ts if flat_group else 8
    block_bytes = ts * (d * 4 + (of.shape[2] + om.shape[2]) * 2 + 2 * d * 4 + d * 4) + u_rows * 2 * d_ff * 4
    const_bytes = sum(c.size * c.dtype.itemsize for c in consts) + const_extra
    temp_bytes = ts * d * 4 * 8 + ts * fc * 4 * 16
    y, u = pl.pallas_call(
        functools.partial(_ffn_kernel, d_ff=d_ff, fc=fc, flat_group=flat_group),
        grid=(B, n // ts), in_specs=in_specs + const_specs,
        out_specs=[row(d), u_spec],
        out_shape=[jax.ShapeDtypeStruct((B, n, d), F32), jax.ShapeDtypeStruct(u_shape, F32)],
        scratch_shapes=scratch,
        compiler_params=pltpu.CompilerParams(
            dimension_semantics=("arbitrary", "arbitrary"),
            vmem_limit_bytes=_vmem_limit(block_bytes, const_bytes, temp_bytes)),
        name="ffn",
    )(*args, *consts)
    return y, u


def _rope_tables(pos):
    half = MLA_ROPE // 2
    inv = ROPE_BASE ** (-jnp.arange(0, MLA_ROPE, 2, dtype=F32) / MLA_ROPE)
    ang = pos.astype(F32)[:, None] * inv[None, :]
    cos, sin = jnp.cos(ang), jnp.sin(ang)
    zero = jnp.zeros_like(sin)
    reps = LANES // MLA_ROPE
    ct = jnp.tile(jnp.concatenate([cos, cos], axis=1), (1, reps))
    sa = jnp.tile(jnp.concatenate([-sin, zero], axis=1), (1, reps))
    sb = jnp.tile(jnp.concatenate([zero, sin], axis=1), (1, reps))
    assert half * 2 == MLA_ROPE
    return ct, sa, sb


def _lower_tri(n, group):
    i = np.arange(n)
    m = (i[None, :] <= i[:, None]) & ((i[None, :] // group) == (i[:, None] // group))
    return jnp.asarray(m, BF16)


def _prepare_weights(norm_mix_g, w_in, b_forget, mla_q_norm_g, w_uq, mla_kv_norm_g, w_ukv, w_o_fox, w_o_mla, w_out,
                     norm_ffn_g, w_up, conv_w, conv_b, w_down, norm_final_g):
    d = w_in.shape[0]
    q_lora, kv_lora = mla_q_norm_g.shape[0], mla_kv_norm_g.shape[0]
    off_ff = 3 * FOX_W
    off_cq = off_ff + FOX_HEADS
    off_ckv = off_cq + q_lora
    off_kr = off_ckv + kv_lora
    off_gate = off_kr + MLA_ROPE
    pad = LANES - F_PARTS * FOX_HEADS
    w_ff = jnp.pad(jnp.tile(w_in[:, off_ff:off_cq], (1, F_PARTS)), ((0, 0), (0, pad)))
    w_a = jnp.concatenate([
        w_in[:, :off_ff], w_ff, w_in[:, off_cq:off_kr],
        jnp.tile(w_in[:, off_kr:off_gate], (1, LANES // MLA_ROPE)),
        w_in[:, off_gate:]], axis=1).astype(BF16)
    uq = w_uq.reshape(q_lora, MLA_HEADS, MLA_QK)
    w_uq_p = jnp.concatenate([uq[:, :, :MLA_NOPE].reshape(q_lora, -1), uq[:, :, MLA_NOPE:].reshape(q_lora, -1)], axis=1)
    ukv = w_ukv.reshape(kv_lora, MLA_HEADS, MLA_NOPE + MLA_V)
    w_ukv_p = jnp.concatenate([ukv[:, :, :MLA_NOPE].reshape(kv_lora, -1), ukv[:, :, MLA_NOPE:].reshape(kv_lora, -1)], axis=1)
    return dict(
        g_mix=norm_mix_g.reshape(1, d), w_a=w_a,
        b_exp=jnp.pad(jnp.tile(b_forget, F_PARTS), (0, pad)).reshape(1, LANES),
        g_q=mla_q_norm_g.reshape(1, q_lora), w_uq=w_uq_p.astype(BF16),
        g_kv=mla_kv_norm_g.reshape(1, kv_lora), w_ukv=w_ukv_p.astype(BF16),
        w_o_fox=w_o_fox.astype(BF16), w_o_mla=w_o_mla.astype(BF16), w_out=w_out.astype(BF16),
        g_ffn=norm_ffn_g.reshape(1, d), w_up=w_up.astype(BF16), conv_w=conv_w, conv_b=conv_b.reshape(1, -1),
        w_down=w_down.astype(BF16), g_fin=norm_final_g.reshape(1, d))


PROJ_ROWS = 512
FFN_ROWS = 256
PROMPT_TQ = 256
PROMPT_PAIRS = 4
SMALL_ROWS = LANES
SMALL_PAIRS = 4


def kernel(x_prompt, x_sample, cache_fox_k, cache_fox_v, cache_fox_logf, cache_mla_ckv, cache_mla_krope, state_ffn_conv, meta_tokens, norm_mix_g, w_in, b_forget, mla_q_norm_g, w_uq, mla_kv_norm_g, w_ukv, w_o_fox, w_o_mla, w_out, norm_ffn_g, w_up, conv_w, conv_b, w_down, norm_final_g):
    assert w_in.shape[0] == 1, "single-layer model"
    B, S, d = x_prompt.shape
    Bs, ns, _ = x_sample.shape
    P = cache_fox_k.shape[2]
    n_meta = meta_tokens.shape[0]
    assert n_meta == N_META == ns
    wts = _prepare_weights(norm_mix_g[0], w_in[0], b_forget[0], mla_q_norm_g[0], w_uq[0], mla_kv_norm_g[0], w_ukv[0],
                           w_o_fox[0], w_o_mla[0], w_out[0], norm_ffn_g[0], w_up[0], conv_w[0], conv_b[0], w_down[0],
                           norm_final_g)
    d_ff2 = w_up.shape[2]

    r_pad = -(-(Bs + 1) * ns // SMALL_ROWS) * SMALL_ROWS
    n_groups = r_pad // ns
    n_fill = r_pad - (Bs + 1) * ns
    rows = jnp.concatenate([x_sample.reshape(Bs * ns, d), jnp.zeros((n_fill, d), x_sample.dtype),
                            meta_tokens.astype(x_sample.dtype)], axis=0)[None]
    pos_small = jnp.concatenate([jnp.tile(P + jnp.arange(ns), Bs), jnp.zeros((n_fill,), jnp.int32), jnp.arange(n_meta)])
    zero_f = jnp.zeros((1, 8, LANES), F32)
    sm = _project(rows, _rope_tables(pos_small), zero_f, wts, _lower_tri(r_pad, ns), ts=r_pad, cs=r_pad)
    sm_b = {k: (jnp.swapaxes(v.reshape(v.shape[1], n_groups, ns), 0, 1) if k in TRANSPOSED
                else v.reshape(n_groups, ns, v.shape[-1])) for k, v in sm.items() if k != 'ftot'}
    live = lambda a: jnp.concatenate([a[:Bs], a[n_groups - 1:]], axis=0)

    feat_major = lambda a: jnp.moveaxis(a[0], 1, -1).reshape(Bs, -1, P)
    prefix_s = _prefix_operands(
        feat_major(cache_fox_k), feat_major(cache_fox_v), feat_major(cache_fox_logf), cache_mla_ckv[0],
        feat_major(cache_mla_krope), wts['w_ukv'], _lower_tri(MXU_DIM, MXU_DIM), cs=MXU_DIM)
    of_s, om_s = _attention({k: live(v) for k, v in sm_b.items()}, prefix_s, tq=ns, tk=ns, tkp=P, pps=SMALL_PAIRS, n_pref=Bs)

    def spread(a):
        return jnp.concatenate([a[:Bs].reshape(Bs * ns, -1), jnp.zeros((n_fill, a.shape[-1]), a.dtype), a[Bs]], axis=0)[None]

    left_groups = jnp.concatenate([state_ffn_conv[0], jnp.zeros((n_groups - Bs, CONV_W - 1, d_ff2), F32)], axis=0)
    y_s, u_s = _merge_ffn(rows, spread(of_s), spread(om_s), sm['gate'], left_groups, wts, ts=SMALL_ROWS, flat_group=ns)
    y_sample = y_s[0, :Bs * ns].reshape(Bs, ns, d)
    u_groups = u_s.reshape(n_groups, ns, d_ff2)
    conv_s = u_groups[:Bs, ns - (CONV_W - 1):]
    left_meta = u_groups[n_groups - 1:, ns - (CONV_W - 1):]

    meta_ops = {k: sm_b[k][n_groups - 1:] for k in ('kfa', 'vf', 'kc', 'vm')}
    pos_p = n_meta + jnp.arange(S)
    pr = _project(x_prompt, _rope_tables(pos_p), sm['ftot'], wts, _lower_tri(MXU_DIM, MXU_DIM), ts=PROJ_ROWS,
                  cs=MXU_DIM, head_rows={k: sm_b[k][n_groups - 1] for k in CACHE_KEYS})
    of_p, om_p = _attention(pr, meta_ops, tq=PROMPT_TQ, tk=PROMPT_TQ, tkp=n_meta, pps=PROMPT_PAIRS)
    y_prompt, conv_p = _merge_ffn(x_prompt, of_p, om_p, pr['gate'], left_meta, wts, ts=FFN_ROWS)
    caches_p = [pr[k].reshape((1, B, n_meta + S) + ((FOX_HEADS, FOX_HEAD_DIM) if k in ('fk', 'fv') else pr[k].shape[3:]))
                for k in CACHE_KEYS]

    def sample_rows(name, tail):
        return sm_b[name][:Bs].reshape((1, Bs, ns) + tail)

    hd = (FOX_HEADS, FOX_HEAD_DIM)
    return (y_prompt, y_sample, *caches_p, conv_p[None],
            sample_rows('fk', hd), sample_rows('fv', hd), sample_rows('logf', (FOX_HEADS,)),
            sample_rows('ckv', (cache_mla_ckv.shape[-1],)), sample_rows('kr', (MLA_ROPE,)), conv_s[None])
```
